```python
import math
import jax, jax.numpy as jnp
from jax import lax
import numpy as np

D_MODEL = 1024
BATCH = 4
SEQ = 4096
DEPTH = 1
DEC_BATCH = 128
DEC_SEQ = 8
PAST_LEN = 8192
PAGE_SIZE = 128

A_HEADS = 8
A_HEAD_DIM = 64
A_WIDTH = A_HEADS * A_HEAD_DIM
MOBA_BLOCK = 256
MOBA_TOPK = 3
Q_BLOCK = 128
ROPE_THETA = 10000.0
G_HEADS = 4
G_DK = 64
G_DV = 128
G_KEY_WIDTH = G_HEADS * G_DK
G_VAL_WIDTH = G_HEADS * G_DV
G_GATE_RANK = 16
G_GATE_NORMALIZER = 16.0
G_CHUNK = 64
P_HEADS = 8
P_N_KEYS = 128
P_N_EXPERTS = P_N_KEYS * P_N_KEYS
P_HALF = 128
P_KEY_DIM = 2 * P_HALF
P_TOPK = 16
P_TOKEN_CHUNK = 256
EPS = 1e-6

IN_SIZES = (A_WIDTH, A_WIDTH, A_WIDTH,
            G_KEY_WIDTH, G_KEY_WIDTH, G_VAL_WIDTH,
            G_GATE_RANK, G_VAL_WIDTH,
            D_MODEL, D_MODEL)
IN_WIDTH = 3 * A_WIDTH + 2 * G_KEY_WIDTH + 2 * G_VAL_WIDTH + G_GATE_RANK + 2 * D_MODEL

kernel_name = "moba_gla_peer_hybrid_step"


def _rms_norm(x, g):
    xf = x.astype(jnp.float32)
    y = xf * lax.rsqrt(jnp.mean(xf * xf, axis=-1, keepdims=True) + EPS)
    return (y * g.astype(jnp.float32)).astype(x.dtype)


def _rope(x, pos):
    half = x.shape[-1] // 2
    inv_freq = ROPE_THETA ** (-jnp.arange(half, dtype=jnp.float32) / half)
    ang = pos.astype(jnp.float32)[:, None] * inv_freq[None, :]
    cos = jnp.cos(ang)[None, :, None, :]
    sin = jnp.sin(ang)[None, :, None, :]
    xf = x.astype(jnp.float32)
    x1, x2 = xf[..., :half], xf[..., half:]
    return jnp.concatenate([x1 * cos - x2 * sin, x2 * cos + x1 * sin], axis=-1).astype(x.dtype)


def _heads(t, n):
    return t.reshape(t.shape[0], t.shape[1], n, -1)


def _mixer_inputs(x, pos, norm1_g, w_in, a_q_norm, a_k_norm, g_gate_w2, g_gate_b):
    z = _rms_norm(x, norm1_g) @ w_in
    parts, off = [], 0
    for size in IN_SIZES:
        parts.append(z[..., off:off + size])
        off += size
    aq, ak, av, gq, gk, gv, glr, gout, ma, mb = parts
    q = _rope(_rms_norm(_heads(aq, A_HEADS), a_q_norm), pos)
    k = _rope(_rms_norm(_heads(ak, A_HEADS), a_k_norm), pos)
    v = _heads(av, A_HEADS)
    gq = _heads(gq, G_HEADS) * (G_DK ** -0.5)
    gk = _heads(gk, G_HEADS)
    gv = _heads(gv, G_HEADS)
    logg = jax.nn.log_sigmoid((glr @ g_gate_w2 + g_gate_b).astype(jnp.float32)) / G_GATE_NORMALIZER
    return q, k, v, gq, gk, gv, _heads(logg, G_HEADS), gout, ma, mb


def _joint_softmax(scores, masks):
    s = jnp.concatenate([jnp.where(m, sc, -jnp.inf) for sc, m in zip(scores, masks)], axis=-1)
    p = jax.nn.softmax(s, axis=-1)
    out, off = [], 0
    for sc in scores:
        out.append(p[..., off:off + sc.shape[-1]])
        off += sc.shape[-1]
    return out


def _moba_select(q, kmeans, own_blk, n_sel):
    s = jnp.einsum('bthd,bnhd->bthn', q.astype(jnp.float32), kmeans)
    past = jnp.arange(kmeans.shape[1])[None, None, None, :] < own_blk[None, :, None, None]
    _, idx = lax.top_k(jnp.where(past, s, -jnp.inf), n_sel)
    valid = idx < own_blk[None, :, None, None]
    return idx, valid


def _moba_prompt(q, k, v):
    B, S, H, dh = q.shape
    nb = -(-S // MOBA_BLOCK)
    pad = nb * MOBA_BLOCK - S
    kb = jnp.pad(k, ((0, 0), (0, pad), (0, 0), (0, 0))).reshape(B, nb, MOBA_BLOCK, H, dh)
    vb = jnp.pad(v, ((0, 0), (0, pad), (0, 0), (0, 0))).reshape(B, nb, MOBA_BLOCK, H, dh)
    n_sel = min(MOBA_TOPK, nb - 1)
    nqb = S // Q_BLOCK
    item = jnp.arange(B * nqb, dtype=jnp.int32)
    xs = (q.reshape(B * nqb, Q_BLOCK, H, dh), item // nqb, (item % nqb) * Q_BLOCK)
    if n_sel > 0:
        kmeans = jnp.mean(kb[:, :nb - 1].astype(jnp.float32), axis=2)
        pos = jnp.arange(S, dtype=jnp.int32)
        idx, valid = _moba_select(q, kmeans, pos // MOBA_BLOCK, n_sel)
        xs = xs + (idx.reshape(B * nqb, Q_BLOCK, H, n_sel), valid.reshape(B * nqb, Q_BLOCK, H, n_sel))
    scale = dh ** -0.5
    hidx = jnp.arange(H)[None, :, None]

    def q_block(args):
        qi, bi, st = args[0], args[1], args[2]
        qf = qi.astype(jnp.float32) * scale
        kb_b, vb_b = kb[bi], vb[bi]
        ob = st // MOBA_BLOCK
        qpos = st + jnp.arange(Q_BLOCK)
        kpos = ob * MOBA_BLOCK + jnp.arange(MOBA_BLOCK)
        k_own = kb_b[ob].astype(jnp.float32)
        v_own = vb_b[ob].astype(jnp.float32)
        scores = [jnp.einsum('qhd,khd->qhk', qf, k_own)]
        masks = [(kpos[None, :] <= qpos[:, None])[:, None, :]]
        if n_sel > 0:
            idx_i, val_i = args[3], args[4]
            k_sel = kb_b[idx_i, :, hidx].reshape(Q_BLOCK, H, n_sel * MOBA_BLOCK, dh).astype(jnp.float32)
            v_sel = vb_b[idx_i, :, hidx].reshape(Q_BLOCK, H, n_sel * MOBA_BLOCK, dh).astype(jnp.float32)
            scores.append(jnp.einsum('qhd,qhld->qhl', qf, k_sel))
            masks.append(jnp.repeat(val_i, MOBA_BLOCK, axis=-1))
        probs = _joint_softmax(scores, masks)
        out = jnp.einsum('qhk,khd->qhd', probs[0], v_own)
        if n_sel > 0:
            out = out + jnp.einsum('qhl,qhld->qhd', probs[1], v_sel)
        return out

    return lax.map(q_block, xs).reshape(B, S, H, dh)


def _moba_sample(q, k_new, v_new, cache_k, cache_v, page_table):
    DB, T, H, dh = q.shape
    ppb = MOBA_BLOCK // PAGE_SIZE
    ob = PAST_LEN // MOBA_BLOCK
    n_tail_pages = (PAST_LEN - ob * MOBA_BLOCK) // PAGE_SIZE
    k_loc, v_loc = k_new, v_new
    if n_tail_pages > 0:
        tail = page_table[:, ob * ppb: ob * ppb + n_tail_pages]
        k_loc = jnp.concatenate([cache_k[tail].reshape(DB, -1, H, dh).astype(k_new.dtype), k_new], axis=1)
        v_loc = jnp.concatenate([cache_v[tail].reshape(DB, -1, H, dh).astype(v_new.dtype), v_new], axis=1)
    scale = dh ** -0.5
    qf = q.astype(jnp.float32) * scale
    qpos = PAST_LEN + jnp.arange(T)
    kpos = ob * MOBA_BLOCK + jnp.arange(k_loc.shape[1])
    scores = [jnp.einsum('bthd,blhd->bthl', qf, k_loc.astype(jnp.float32))]
    masks = [(kpos[None, :] <= qpos[:, None])[None, :, None, :]]
    n_sel = min(MOBA_TOPK, ob)
    if n_sel > 0:
        k_past = cache_k[page_table[:, :ob * ppb]].reshape(DB, ob, MOBA_BLOCK, H, dh)
        kmeans = jnp.mean(k_past.astype(jnp.float32), axis=2)
        idx, valid = _moba_select(q, kmeans, jnp.full((T,), ob, jnp.int32), n_sel)
        lp = idx[..., None] * ppb + jnp.arange(ppb)
        phys = page_table[jnp.arange(DB)[:, None, None, None, None], lp]
        hidx = jnp.arange(H)[None, None, :, None, None]
        k_sel = cache_k[phys, :, hidx].reshape(DB, T, H, n_sel * MOBA_BLOCK, dh).astype(jnp.float32)
        v_sel = cache_v[phys, :, hidx].reshape(DB, T, H, n_sel * MOBA_BLOCK, dh).astype(jnp.float32)
        scores.append(jnp.einsum('bthd,bthld->bthl', qf, k_sel))
        masks.append(jnp.repeat(valid, MOBA_BLOCK, axis=-1))
    probs = _joint_softmax(scores, masks)
    out = jnp.einsum('bthl,blhd->bthd', probs[0], v_loc.astype(jnp.float32))
    if n_sel > 0:
        out = out + jnp.einsum('bthl,bthld->bthd', probs[1], v_sel)
    return out


def _gla(q, k, v, logg, s0):
    B, T, H, dk = q.shape
    dv = v.shape[-1]
    C = math.gcd(T, G_CHUNK)
    n = T // C

    def chunks(a):
        return a.astype(jnp.float32).reshape(B, n, C, H, a.shape[-1]).transpose(1, 0, 3, 2, 4)

    causal = jnp.tril(jnp.ones((C, C), dtype=bool))[None, None, :, :, None]

    def step(state, inp):
        qc, kc, vc, gc = inp
        b = jnp.cumsum(gc, axis=2)
        o_inter = jnp.einsum('bhtd,bhde->bhte', qc * jnp.exp(b), state)
        rel = jnp.where(causal, b[:, :, :, None, :] - b[:, :, None, :, :], -jnp.inf)
        attn = jnp.einsum('bhtd,bhtsd,bhsd->bhts', qc, jnp.exp(rel), kc)
        o = o_inter + jnp.einsum('bhts,bhse->bhte', attn, vc)
        b_last = b[:, :, -1, :]
        state = jnp.exp(b_last)[..., None] * state + jnp.einsum(
            'bhsd,bhse->bhde', kc * jnp.exp(b_last[:, :, None, :] - b), vc)
        return state, o

    s_fin, o = lax.scan(step, s0.astype(jnp.float32), (chunks(q), chunks(k), chunks(v), chunks(logg)))
    return o.transpose(1, 0, 3, 2, 4).reshape(B, T, H, dv), s_fin


def _peer(h, wq, sk1, sk2, u_tab, v_tab):
    B, T, D = h.shape
    n = B * T
    hf = h.reshape(n, D)
    q = (hf @ wq).astype(jnp.float32).reshape(n, P_HEADS, 2, P_HALF)
    s1 = jnp.einsum('nhc,hkc->nhk', q[:, :, 0], sk1.astype(jnp.float32))
    s2 = jnp.einsum('nhc,hkc->nhk', q[:, :, 1], sk2.astype(jnp.float32))
    v1, i1 = lax.top_k(s1, P_TOPK)
    v2, i2 = lax.top_k(s2, P_TOPK)
    cand = (v1[..., :, None] + v2[..., None, :]).reshape(n, P_HEADS, P_TOPK * P_TOPK)
    cid = (i1[..., :, None] * P_N_KEYS + i2[..., None, :]).reshape(n, P_HEADS, P_TOPK * P_TOPK)
    top, at = lax.top_k(cand, P_TOPK)
    eid = jnp.take_along_axis(cid, at, axis=-1)
    gate = jax.nn.softmax(top, axis=-1)
    n_pad = -(-n // P_TOKEN_CHUNK) * P_TOKEN_CHUNK
    padn = n_pad - n
    xs = jnp.pad(hf, ((0, padn), (0, 0))).reshape(-1, P_TOKEN_CHUNK, D)
    es = jnp.pad(eid, ((0, padn), (0, 0), (0, 0))).reshape(-1, P_TOKEN_CHUNK, P_HEADS, P_TOPK)
    gs = jnp.pad(gate, ((0, padn), (0, 0), (0, 0))).reshape(-1, P_TOKEN_CHUNK, P_HEADS, P_TOPK)

    def token_chunk(args):
        xc, ec, gc = args
        act = jax.nn.gelu(jnp.einsum('cd,chkd->chk', xc.astype(jnp.float32),
                                     u_tab[ec].astype(jnp.float32)), approximate=False)
        return jnp.einsum('chk,chkd->cd', gc * act, v_tab[ec].astype(jnp.float32))

    out = lax.map(token_chunk, (xs, es, gs)).reshape(n_pad, D)[:n]
    return out.reshape(B, T, D).astype(h.dtype)


def _block_output(x, o_a, o_b, gout, ma, mb, g_out_norm, w_branch_a, w_branch_b, w_out,
                  norm2_g, peer_wq, peer_sk1, peer_sk2, peer_u, peer_v):
    B, T, _ = x.shape
    y_a = o_a.reshape(B, T, A_WIDTH).astype(x.dtype) @ w_branch_a
    o_b = _rms_norm(o_b, g_out_norm) * jax.nn.silu(_heads(gout, G_HEADS).astype(jnp.float32))
    y_b = o_b.reshape(B, T, G_VAL_WIDTH).astype(x.dtype) @ w_branch_b
    merged = jax.nn.sigmoid(ma) * y_a + jax.nn.sigmoid(mb) * y_b
    h = x + merged @ w_out
    return h + _peer(_rms_norm(h, norm2_g), peer_wq, peer_sk1, peer_sk2, peer_u, peer_v)


def setup_inputs(seed: int = 0) -> dict:
    key = jax.random.key(seed)
    ks = jax.random.split(key, 24)
    f32 = jnp.float32
    n_pages = PAST_LEN // PAGE_SIZE
    n_phys = (5 * DEC_BATCH * n_pages + 3) // 4

    def nrm(k, shape, scale):
        return jax.random.normal(k, shape, f32) * scale

    page_table = jax.random.permutation(ks[5], n_phys)[:DEC_BATCH * n_pages]
    page_table = page_table.reshape(DEC_BATCH, n_pages).astype(jnp.int32)
    return {
        "x_prompt": nrm(ks[0], (BATCH, SEQ, D_MODEL), 1.0),
        "x_sample": nrm(ks[1], (DEC_BATCH, DEC_SEQ, D_MODEL), 1.0),
        "cache_k": nrm(ks[2], (DEPTH, n_phys, PAGE_SIZE, A_HEADS, A_HEAD_DIM), 1.0),
        "cache_v": nrm(ks[3], (DEPTH, n_phys, PAGE_SIZE, A_HEADS, A_HEAD_DIM), 1.0),
        "state_gla": nrm(ks[4], (DEPTH, DEC_BATCH, G_HEADS, G_DK, G_DV), 0.3),
        "page_table": page_table,
        "norm1_g": 1.0 + nrm(ks[6], (DEPTH, D_MODEL), 0.02),
        "w_in": nrm(ks[7], (DEPTH, D_MODEL, IN_WIDTH), D_MODEL ** -0.5),
        "a_q_norm": 1.0 + nrm(ks[8], (DEPTH, A_HEAD_DIM), 0.02),
        "a_k_norm": 1.0 + nrm(ks[9], (DEPTH, A_HEAD_DIM), 0.02),
        "g_gate_w2": nrm(ks[10], (DEPTH, G_GATE_RANK, G_KEY_WIDTH), G_GATE_RANK ** -0.5),
        "g_gate_b": nrm(ks[11], (DEPTH, G_KEY_WIDTH), 0.1),
        "g_out_norm": 1.0 + nrm(ks[12], (DEPTH, G_DV), 0.02),
        "w_branch_a": nrm(ks[13], (DEPTH, A_WIDTH, D_MODEL), A_WIDTH ** -0.5),
        "w_branch_b": nrm(ks[14], (DEPTH, G_VAL_WIDTH, D_MODEL), G_VAL_WIDTH ** -0.5),
        "w_out": nrm(ks[15], (DEPTH, D_MODEL, D_MODEL), D_MODEL ** -0.5),
        "norm2_g": 1.0 + nrm(ks[16], (DEPTH, D_MODEL), 0.02),
        "peer_wq": nrm(ks[17], (DEPTH, D_MODEL, P_HEADS * P_KEY_DIM), D_MODEL ** -0.5),
        "peer_subkeys1": nrm(ks[18], (DEPTH, P_HEADS, P_N_KEYS, P_HALF), P_HALF ** -0.5),
        "peer_subkeys2": nrm(ks[19], (DEPTH, P_HEADS, P_N_KEYS, P_HALF), P_HALF ** -0.5),
        "peer_u": nrm(ks[20], (DEPTH, P_N_EXPERTS, D_MODEL), D_MODEL ** -0.5),
        "peer_v": nrm(ks[21], (DEPTH, P_N_EXPERTS, D_MODEL), P_HEADS ** -0.5),
    }


def reference(x_prompt, x_sample, cache_k, cache_v, state_gla, page_table, norm1_g, w_in,
              a_q_norm, a_k_norm, g_gate_w2, g_gate_b, g_out_norm, w_branch_a, w_branch_b,
              w_out, norm2_g, peer_wq, peer_subkeys1, peer_subkeys2, peer_u, peer_v):
    pos_p = jnp.arange(x_prompt.shape[1], dtype=jnp.int32)
    pos_s = PAST_LEN + jnp.arange(x_sample.shape[1], dtype=jnp.int32)
    hp, hs = x_prompt, x_sample
    k_p, v_p, st_p, k_s, v_s, st_s = [], [], [], [], [], []
    for l in range(DEPTH):
        mix = (norm1_g[l], w_in[l], a_q_norm[l], a_k_norm[l], g_gate_w2[l], g_gate_b[l])
        post = (g_out_norm[l], w_branch_a[l], w_branch_b[l], w_out[l], norm2_g[l],
                peer_wq[l], peer_subkeys1[l], peer_subkeys2[l], peer_u[l], peer_v[l])
        q, k, v, gq, gk, gv, logg, gout, ma, mb = _mixer_inputs(hp, pos_p, *mix)
        o_a = _moba_prompt(q, k, v)
        s0 = jnp.zeros((hp.shape[0], G_HEADS, G_DK, G_DV), jnp.float32)
        o_b, s_fin = _gla(gq, gk, gv, logg, s0)
        hp = _block_output(hp, o_a, o_b, gout, ma, mb, *post)
        k_p.append(k)
        v_p.append(v)
        st_p.append(s_fin.astype(state_gla.dtype))
        q, k, v, gq, gk, gv, logg, gout, ma, mb = _mixer_inputs(hs, pos_s, *mix)
        o_a = _moba_sample(q, k, v, cache_k[l], cache_v[l], page_table)
        o_b, s_fin = _gla(gq, gk, gv, logg, state_gla[l])
        hs = _block_output(hs, o_a, o_b, gout, ma, mb, *post)
        k_s.append(k)
        v_s.append(v)
        st_s.append(s_fin.astype(state_gla.dtype))
    return (hp, hs, jnp.stack(k_p), jnp.stack(v_p), jnp.stack(st_p),
            jnp.stack(k_s), jnp.stack(v_s), jnp.stack(st_s))
```

```python
import functools
import math

import jax
import jax.numpy as jnp
from jax import lax
from jax.experimental import pallas as pl
from jax.experimental.pallas import tpu as pltpu

F32 = jnp.float32
BF16 = jnp.bfloat16

D_MODEL = 1024
PAST_LEN = 8192
PAGE_SIZE = 128
A_HEADS = 8
A_HEAD_DIM = 64
A_WIDTH = A_HEADS * A_HEAD_DIM
MOBA_BLOCK = 256
MOBA_TOPK = 3
ROPE_THETA = 10000.0
G_HEADS = 4
G_DK = 64
G_DV = 128
G_KEY_WIDTH = G_HEADS * G_DK
G_VAL_WIDTH = G_HEADS * G_DV
G_GATE_RANK = 16
G_GATE_NORMALIZER = 16.0
G_CHUNK = 64
G_SUB = 16
P_HEADS = 8
P_N_KEYS = 128
P_HALF = 128
P_TOPK = 16
EPS = 1e-6

LANES = 128
VMEM_LIMIT = 56 * 1024 * 1024

TOKEN_TILE = 256
GLR_PAD = LANES

NEG = -1e30


def _nt(a, b):
    return lax.dot_general(a, b, (((1,), (1,)), ((), ())), preferred_element_type=F32)


def _tn(a, b):
    return lax.dot_general(a, b, (((0,), (0,)), ((), ())), preferred_element_type=F32)


def _dot(a, b):
    return jnp.dot(a, b, preferred_element_type=F32)


def _split2(x):
    hi = x.astype(BF16)
    lo = (x - hi.astype(F32)).astype(BF16)
    return hi, lo


def _split3(x):
    hi = x.astype(BF16)
    r = x - hi.astype(F32)
    mid = r.astype(BF16)
    lo = (r - mid.astype(F32)).astype(BF16)
    return hi, mid, lo


_SEG_Q, _SEG_K, _SEG_V = 0, 512, 1024
_SEG_GQ, _SEG_GK, _SEG_GV = 1536, 1792, 2048
_SEG_GLR = 2560
_SEG_GOUT = _SEG_GLR + GLR_PAD
_SEG_MA = _SEG_GOUT + 512
_SEG_MB = _SEG_MA + 1024
_W_IN_PADDED = _SEG_MB + 1024


def _inproj_kernel(x_ref, g1_ref, w_ref, cos_ref, sin_ref, qn_ref, kn_ref, bd_ref, w2_ref, gb_ref,
                   q_ref, k_ref, v_ref, gq_ref, gk_ref, gv_ref, lg_ref, go_ref, ma_ref, mb_ref, km_ref):
    x = x_ref[...]
    xn = x * lax.rsqrt(jnp.mean(x * x, axis=-1, keepdims=True) + EPS) * g1_ref[...]
    xb = xn.astype(BF16)

    def proj(off, width):
        return _dot(xb, w_ref[:, off:off + width])

    cos = cos_ref[...]
    sin = sin_ref[...]
    lane = lax.broadcasted_iota(jnp.int32, (1, A_WIDTH), 1)
    first_half = (lane % A_HEAD_DIM) < (A_HEAD_DIM // 2)
    bd = bd_ref[...]

    def qk_epilogue(z, gamma):
        hi, lo = _split2(z * z)
        ms = (_dot(hi, bd) + _dot(lo, bd)) * (1.0 / A_HEAD_DIM)
        y = z * lax.rsqrt(ms + EPS) * gamma
        rot = jnp.where(first_half,
                        -pltpu.roll(y, A_WIDTH - A_HEAD_DIM // 2, 1),
                        pltpu.roll(y, A_HEAD_DIM // 2, 1))
        return y * cos + rot * sin

    q_ref[...] = qk_epilogue(proj(_SEG_Q, A_WIDTH), qn_ref[...])
    k = qk_epilogue(proj(_SEG_K, A_WIDTH), kn_ref[...])
    k_ref[...] = k
    km_ref[0] = jnp.mean(k, axis=0, keepdims=True)
    v_ref[...] = proj(_SEG_V, A_WIDTH)
    gq_ref[...] = proj(_SEG_GQ, G_KEY_WIDTH) * (G_DK ** -0.5)
    gk_ref[...] = proj(_SEG_GK, G_KEY_WIDTH)
    gv_ref[...] = proj(_SEG_GV, G_VAL_WIDTH)
    glr = proj(_SEG_GLR, GLR_PAD)
    t = _dot(glr.astype(BF16), w2_ref[...]) + gb_ref[...]
    lg_ref[...] = (jnp.minimum(t, 0.0) - jnp.log1p(jnp.exp(-jnp.abs(t)))) * (1.0 / G_GATE_NORMALIZER)
    go_ref[...] = proj(_SEG_GOUT, G_VAL_WIDTH)
    ma_ref[...] = proj(_SEG_MA, D_MODEL)
    mb_ref[...] = proj(_SEG_MB, D_MODEL)


def _inproj(x, pos_block_of_tile, cos_tab, sin_tab, p):
    n = x.shape[0]
    tm = TOKEN_TILE
    nt = n // tm
    row = lambda i: (i, 0)
    const = lambda i: (0, 0)
    widths = (A_WIDTH, A_WIDTH, A_WIDTH, G_KEY_WIDTH, G_KEY_WIDTH, G_VAL_WIDTH, G_KEY_WIDTH,
              G_VAL_WIDTH, D_MODEL, D_MODEL)
    out_shape = [jax.ShapeDtypeStruct((n, w), F32) for w in widths]
    out_specs = [pl.BlockSpec((tm, w), row) for w in widths]
    out_shape.append(jax.ShapeDtypeStruct((nt, 1, A_WIDTH), F32))
    out_specs.append(pl.BlockSpec((1, 1, A_WIDTH), lambda i: (i, 0, 0)))
    tab_spec = pl.BlockSpec((tm, A_WIDTH), lambda i: (pos_block_of_tile(i), 0))
    return pl.pallas_call(
        _inproj_kernel,
        grid=(nt,),
        in_specs=[
            pl.BlockSpec((tm, D_MODEL), row),
            pl.BlockSpec((1, D_MODEL), const),
            pl.BlockSpec((D_MODEL, _W_IN_PADDED), const),
            tab_spec, tab_spec,
            pl.BlockSpec((1, A_WIDTH), const),
            pl.BlockSpec((1, A_WIDTH), const),
            pl.BlockSpec((A_WIDTH, A_WIDTH), const),
            pl.BlockSpec((GLR_PAD, G_KEY_WIDTH), const),
            pl.BlockSpec((1, G_KEY_WIDTH), const),
        ],
        out_specs=out_specs,
        out_shape=out_shape,
        compiler_params=pltpu.CompilerParams(dimension_semantics=("arbitrary",), vmem_limit_bytes=VMEM_LIMIT),
        name="inproj",
    )(x, p["g1"], p["w_in"], cos_tab, sin_tab, p["qn"], p["kn"], p["bd"], p["w2"], p["gb"])


def _rope_tables(pos):
    half = A_HEAD_DIM // 2
    inv_freq = ROPE_THETA ** (-jnp.arange(half, dtype=F32) / half)
    ang = pos.astype(F32)[:, None] * inv_freq[None, :]
    reps = A_WIDTH // half
    return jnp.tile(jnp.cos(ang), (1, reps)), jnp.tile(jnp.sin(ang), (1, reps))


def _prep_mixer_params(norm1_g, w_in, a_q_norm, a_k_norm, g_gate_w2, g_gate_b):
    glr_off = 3 * A_WIDTH + 2 * G_KEY_WIDTH + G_VAL_WIDTH
    w_pad = jnp.concatenate([
        w_in[:, :glr_off],
        jnp.pad(w_in[:, glr_off:glr_off + G_GATE_RANK], ((0, 0), (0, GLR_PAD - G_GATE_RANK))),
        w_in[:, glr_off + G_GATE_RANK:]], axis=1).astype(BF16)
    head_id = jnp.arange(A_WIDTH) // A_HEAD_DIM
    return {
        "g1": norm1_g.reshape(1, D_MODEL),
        "w_in": w_pad,
        "qn": jnp.tile(a_q_norm, A_HEADS).reshape(1, A_WIDTH),
        "kn": jnp.tile(a_k_norm, A_HEADS).reshape(1, A_WIDTH),
        "bd": (head_id[:, None] == head_id[None, :]).astype(BF16),
        "w2": jnp.pad(g_gate_w2, ((0, GLR_PAD - G_GATE_RANK), (0, 0))).astype(BF16),
        "gb": g_gate_b.reshape(1, G_KEY_WIDTH),
    }


def _topk_rows_mask(s, valid, n_sel):
    n = s.shape[0]
    sm = jnp.where(valid, s, -jnp.inf)
    ridx = lax.broadcasted_iota(jnp.int32, s.shape, 0)
    cnt = jnp.zeros(s.shape, jnp.int32)
    for r in range(n):
        row = sm[r:r + 1, :]
        beats = (row > sm) | ((row == sm) & (r < ridx))
        cnt = cnt + beats.astype(jnp.int32)
    return valid & (cnt < n_sel)


def _moba_prompt_kernel(q_ref, k_ref, v_ref, km_ref, o_ref, sel_ref):
    ob = pl.program_id(2)
    blk = MOBA_BLOCK
    nb = km_ref.shape[0]
    scale = A_HEAD_DIM ** -0.5
    kpos = lax.broadcasted_iota(jnp.int32, (blk, blk), 0)
    qpos = lax.broadcasted_iota(jnp.int32, (blk, blk), 1)
    causal = kpos <= qpos
    outs = []
    for hh in range(LANES // A_HEAD_DIM):
        sl = slice(hh * A_HEAD_DIM, (hh + 1) * A_HEAD_DIM)
        q = q_ref[:, sl]
        s_sel = lax.dot_general(km_ref[:, sl], q, (((1,), (1,)), ((), ())),
                                precision=lax.Precision.HIGHEST, preferred_element_type=F32)
        nidx = lax.broadcasted_iota(jnp.int32, (nb, blk), 0)
        sel = _topk_rows_mask(s_sel, nidx < ob, MOBA_TOPK)
        sel_ref[...] = sel.astype(F32)
        qs = (q * scale).astype(BF16)

        def scores(n):
            kb = k_ref[pl.ds(n * blk, blk), sl].astype(BF16)
            return _nt(kb, qs)

        def values_t(n):
            return v_ref[pl.ds(n * blk, blk), sl].T.astype(BF16)

        s0 = jnp.where(causal, scores(ob), -jnp.inf)
        m0 = jnp.max(s0, axis=0, keepdims=True)
        p0 = jnp.exp(s0 - m0)
        l0 = jnp.sum(p0, axis=0, keepdims=True)
        acc0 = _dot(values_t(ob), p0.astype(BF16))

        def body(n, carry):
            m, l, acc = carry
            keep = sel_ref[pl.ds(n, 1), :] > 0.5
            s = jnp.where(keep, scores(n), -jnp.inf)
            m_new = jnp.maximum(m, jnp.max(s, axis=0, keepdims=True))
            alpha = jnp.exp(m - m_new)
            p = jnp.exp(s - m_new)
            l = alpha * l + jnp.sum(p, axis=0, keepdims=True)
            acc = alpha * acc + _dot(values_t(n), p.astype(BF16))
            return m_new, l, acc

        _, l, acc = lax.fori_loop(0, ob, body, (m0, l0, acc0))
        outs.append((acc / l).T)
    o_ref[...] = jnp.concatenate(outs, axis=1)


def _moba_prompt(q, k, v, kmeans, batch, seq):
    nb = seq // MOBA_BLOCK
    hp = A_WIDTH // LANES
    return pl.pallas_call(
        _moba_prompt_kernel,
        grid=(batch, hp, nb),
        in_specs=[
            pl.BlockSpec((MOBA_BLOCK, LANES), lambda b, h, i: (b * nb + i, h)),
            pl.BlockSpec((seq, LANES), lambda b, h, i: (b, h)),
            pl.BlockSpec((seq, LANES), lambda b, h, i: (b, h)),
            pl.BlockSpec((nb, LANES), lambda b, h, i: (b, h)),
        ],
        out_specs=pl.BlockSpec((MOBA_BLOCK, LANES), lambda b, h, i: (b * nb + i, h)),
        out_shape=jax.ShapeDtypeStruct((batch * seq, A_WIDTH), F32),
        scratch_shapes=[pltpu.VMEM((nb, MOBA_BLOCK), F32)],
        compiler_params=pltpu.CompilerParams(
            dimension_semantics=("arbitrary", "arbitrary", "arbitrary"), vmem_limit_bytes=VMEM_LIMIT),
        name="moba_prompt",
    )(q, k, v, kmeans)


def _cumsum_rows(tri, g):
    hi, mid, lo = _split3(g)
    return _dot(tri, hi) + _dot(tri, mid) + _dot(tri, lo)


def _gla_prompt_kernel(q_ref, k_ref, v_ref, g_ref, tri_ref, o_ref, s_ref, st_ref):
    t = pl.program_id(1)
    c, sub = G_CHUNK, G_SUB

    @pl.when(t == 0)
    def _():
        st_ref[...] = jnp.zeros_like(st_ref)

    tri = tri_ref[...]
    for ci in range(TOKEN_TILE // c):
        rows = slice(ci * c, (ci + 1) * c)
        q, k, v, g = q_ref[rows, :], k_ref[rows, :], v_ref[rows, :], g_ref[rows, :]
        b = _cumsum_rows(tri, g)
        blast = b[c - 1:c, :]
        q_inter = (q * jnp.exp(b)).astype(BF16)
        k_dec = (k * jnp.exp(blast - b)).astype(BF16)
        vb = v.astype(BF16)
        q_sub, k_sub = [], []
        for i in range(c // sub):
            bref = b[i * sub - 1:i * sub, :] if i > 0 else jnp.zeros((1, G_KEY_WIDTH), F32)
            r = slice(i * sub, (i + 1) * sub)
            ncol = (i + 1) * sub
            q_sub.append((q[r, :] * jnp.exp(b[r, :] - bref)).astype(BF16))
            k_sub.append((k[:ncol, :] * jnp.exp(bref - b[:ncol, :])).astype(BF16))
        decay = jnp.exp(blast)
        outs = []
        for h in range(G_HEADS):
            hs = slice(h * G_DK, (h + 1) * G_DK)
            vs = slice(h * G_DV, (h + 1) * G_DV)
            st = st_ref[h]
            o = _nt(q_inter[:, hs], st.astype(BF16))
            intra = []
            for i in range(c // sub):
                ncol = (i + 1) * sub
                a = _nt(q_sub[i][:, hs], k_sub[i][:, hs])
                rr = lax.broadcasted_iota(jnp.int32, (sub, ncol), 0) + i * sub
                cc = lax.broadcasted_iota(jnp.int32, (sub, ncol), 1)
                a = jnp.where(cc <= rr, a, 0.0).astype(BF16)
                intra.append(_dot(a, vb[:ncol, vs]))
            outs.append(o + jnp.concatenate(intra, axis=0))
            st_ref[h] = st * decay[:, hs] + _tn(vb[:, vs], k_dec[:, hs])
        o_ref[rows, :] = jnp.concatenate(outs, axis=1)

    @pl.when(t == pl.num_programs(1) - 1)
    def _():
        for h in range(G_HEADS):
            s_ref[0, h] = st_ref[h].T


def _gla_prompt(gq, gk, gv, logg, batch, seq):
    nt = seq // TOKEN_TILE
    row = lambda b, t: (b * nt + t, 0)
    tri = jnp.tril(jnp.ones((G_CHUNK, G_CHUNK), F32)).astype(BF16)
    return pl.pallas_call(
        _gla_prompt_kernel,
        grid=(batch, nt),
        in_specs=[
            pl.BlockSpec((TOKEN_TILE, G_KEY_WIDTH), row),
            pl.BlockSpec((TOKEN_TILE, G_KEY_WIDTH), row),
            pl.BlockSpec((TOKEN_TILE, G_VAL_WIDTH), row),
            pl.BlockSpec((TOKEN_TILE, G_KEY_WIDTH), row),
            pl.BlockSpec((G_CHUNK, G_CHUNK), lambda b, t: (0, 0)),
        ],
        out_specs=[
            pl.BlockSpec((TOKEN_TILE, G_VAL_WIDTH), row),
            pl.BlockSpec((1, G_HEADS, G_DK, G_DV), lambda b, t: (b, 0, 0, 0)),
        ],
        out_shape=[
            jax.ShapeDtypeStruct((batch * seq, G_VAL_WIDTH), F32),
            jax.ShapeDtypeStruct((batch, G_HEADS, G_DK, G_DV), F32),
        ],
        scratch_shapes=[pltpu.VMEM((G_HEADS, G_DV, G_DK), F32)],
        compiler_params=pltpu.CompilerParams(
            dimension_semantics=("arbitrary", "arbitrary"), vmem_limit_bytes=VMEM_LIMIT),
        name="gla_prompt",
    )(gq, gk, gv, logg, tri)


GLA_SAMPLE_SEQS = 8


def _gla_sample_kernel(q_ref, k_ref, v_ref, g_ref, tri_ref, s0_ref, o_ref, s_ref, *, steps):
    q, k, v, g = q_ref[...], k_ref[...], v_ref[...], g_ref[...]
    b = _cumsum_rows(tri_ref[...], g)
    eb = jnp.exp(b)
    qd = (q * eb).astype(BF16)
    kinv = (k * jnp.exp(-b)).astype(BF16)
    vb = v.astype(BF16)
    rr = lax.broadcasted_iota(jnp.int32, (steps, steps), 0)
    cc = lax.broadcasted_iota(jnp.int32, (steps, steps), 1)
    eye = lax.broadcasted_iota(jnp.int32, (G_DK, G_DK), 0) == lax.broadcasted_iota(jnp.int32, (G_DK, G_DK), 1)
    out_rows = []
    for s in range(GLA_SAMPLE_SEQS):
        rows = slice(s * steps, (s + 1) * steps)
        blast = b[(s + 1) * steps - 1:(s + 1) * steps, :]
        k_dec = (k[rows, :] * jnp.exp(blast - b[rows, :])).astype(BF16)
        outs = []
        for h in range(G_HEADS):
            hs = slice(h * G_DK, (h + 1) * G_DK)
            vs = slice(h * G_DV, (h + 1) * G_DV)
            s0 = s0_ref[s, h]
            a = jnp.where(cc <= rr, _nt(qd[rows, hs], kinv[rows, hs]), 0.0).astype(BF16)
            outs.append(_dot(qd[rows, hs], s0.astype(BF16)) + _dot(a, vb[rows, vs]))
            dcol = jnp.sum(jnp.where(eye, jnp.exp(blast[:, hs]), 0.0), axis=1, keepdims=True)
            s_ref[s, h] = dcol * s0 + _tn(k_dec[:, hs], vb[rows, vs])
        out_rows.append(jnp.concatenate(outs, axis=1))
    o_ref[...] = jnp.concatenate(out_rows, axis=0)


def _gla_sample(gq, gk, gv, logg, state, steps):
    nseq = state.shape[0]
    sb = GLA_SAMPLE_SEQS
    rows = sb * steps
    idx = jnp.arange(rows)
    tri = ((idx[:, None] >= idx[None, :]) & (idx[:, None] // steps == idx[None, :] // steps)).astype(BF16)
    row = lambda i: (i, 0)
    return pl.pallas_call(
        functools.partial(_gla_sample_kernel, steps=steps),
        grid=(nseq // sb,),
        in_specs=[
            pl.BlockSpec((rows, G_KEY_WIDTH), row),
            pl.BlockSpec((rows, G_KEY_WIDTH), row),
            pl.BlockSpec((rows, G_VAL_WIDTH), row),
            pl.BlockSpec((rows, G_KEY_WIDTH), row),
            pl.BlockSpec((rows, rows), lambda i: (0, 0)),
            pl.BlockSpec((sb, G_HEADS, G_DK, G_DV), lambda i: (i, 0, 0, 0)),
        ],
        out_specs=[
            pl.BlockSpec((rows, G_VAL_WIDTH), row),
            pl.BlockSpec((sb, G_HEADS, G_DK, G_DV), lambda i: (i, 0, 0, 0)),
        ],
        out_shape=[
            jax.ShapeDtypeStruct((nseq * steps, G_VAL_WIDTH), F32),
            jax.ShapeDtypeStruct(state.shape, F32),
        ],
        compiler_params=pltpu.CompilerParams(dimension_semantics=("arbitrary",), vmem_limit_bytes=VMEM_LIMIT),
        name="gla_sample",
    )(gq, gk, gv, logg, tri, state)


PAGES_PER_BLOCK = MOBA_BLOCK // PAGE_SIZE


def _moba_sample_kernel(pt_ref, q_ref, kn_ref, vn_ref, k0_ref, k1_ref, v0_ref, v1_ref, o_ref,
                        wq_ref, m_ref, l_ref, ss_ref, acc_ref, *, steps, n_blocks):
    del pt_ref
    n = pl.program_id(1)
    nq = A_HEADS * steps
    scale = A_HEAD_DIM ** -0.5
    lane = lax.broadcasted_iota(jnp.int32, (nq, LANES), 1)

    @pl.when(n == 0)
    def _():
        qt = jnp.concatenate([q_ref[...]] * A_HEADS, axis=0)
        rh = lax.broadcasted_iota(jnp.int32, (nq, A_WIDTH), 0) // steps
        ch = lax.broadcasted_iota(jnp.int32, (nq, A_WIDTH), 1) // A_HEAD_DIM
        wq_ref[...] = jnp.where(rh == ch, qt * scale, 0.0).astype(BF16)
        m_ref[...] = jnp.full(m_ref.shape, -jnp.inf, F32)
        l_ref[...] = jnp.zeros_like(l_ref)
        ss_ref[...] = jnp.full(ss_ref.shape, -jnp.inf, F32)

    wq = wq_ref[...]
    kb = jnp.concatenate([k0_ref[0], k1_ref[0]], axis=0).astype(BF16)
    vb = jnp.concatenate([v0_ref[0], v1_ref[0]], axis=0).astype(BF16)
    s = _nt(wq, kb)
    m = jnp.max(s, axis=1, keepdims=True)
    p = jnp.exp(s - m)
    here = lane == n
    m_ref[...] = jnp.where(here, m, m_ref[...])
    l_ref[...] = jnp.where(here, jnp.sum(p, axis=1, keepdims=True), l_ref[...])
    ss_ref[...] = jnp.where(here, jnp.sum(s, axis=1, keepdims=True), ss_ref[...])
    acc_ref[n] = _dot(p.astype(BF16), vb)

    @pl.when(n == n_blocks - 1)
    def _():
        ss = ss_ref[...]
        valid = lane < n_blocks
        cnt = jnp.zeros((nq, LANES), jnp.int32)
        for j in range(n_blocks):
            col = ss[:, j:j + 1]
            beats = (col > ss) | ((col == ss) & (j < lane))
            cnt = cnt + beats.astype(jnp.int32)
        sel = valid & (cnt < min(MOBA_TOPK, n_blocks))
        s_own = _nt(wq, kn_ref[...].astype(BF16))
        tq = lax.broadcasted_iota(jnp.int32, (nq, steps), 0) % steps
        tk = lax.broadcasted_iota(jnp.int32, (nq, steps), 1)
        s_own = jnp.where(tk <= tq, s_own, -jnp.inf)
        mm = jnp.where(sel, m_ref[...], -jnp.inf)
        big = jnp.maximum(jnp.max(mm, axis=1, keepdims=True), jnp.max(s_own, axis=1, keepdims=True))
        w = jnp.where(sel, jnp.exp(mm - big), 0.0)
        p_own = jnp.exp(s_own - big)
        denom = jnp.sum(w * l_ref[...], axis=1, keepdims=True) + jnp.sum(p_own, axis=1, keepdims=True)
        acc = _dot(p_own.astype(BF16), vn_ref[...].astype(BF16))
        for j in range(n_blocks):
            acc = acc + w[:, j:j + 1] * acc_ref[j]
        acc = acc / denom
        o_ref[...] = jnp.concatenate(
            [acc[h * steps:(h + 1) * steps, h * A_HEAD_DIM:(h + 1) * A_HEAD_DIM] for h in range(A_HEADS)], axis=1)


def _moba_sample(q, k_new, v_new, cache_k, cache_v, page_table, steps):
    nseq, n_pages = page_table.shape
    n_blocks = n_pages // PAGES_PER_BLOCK
    nq = A_HEADS * steps
    tok = pl.BlockSpec((steps, A_WIDTH), lambda s, n, pt: (s, 0))

    def page(j):
        return pl.BlockSpec((1, PAGE_SIZE, A_WIDTH),
                            lambda s, n, pt: (pt[s * n_pages + n * PAGES_PER_BLOCK + j], 0, 0))

    grid_spec = pltpu.PrefetchScalarGridSpec(
        num_scalar_prefetch=1,
        grid=(nseq, n_blocks),
        in_specs=[tok, tok, tok, page(0), page(1), page(0), page(1)],
        out_specs=tok,
        scratch_shapes=[
            pltpu.VMEM((nq, A_WIDTH), BF16),
            pltpu.VMEM((nq, LANES), F32),
            pltpu.VMEM((nq, LANES), F32),
            pltpu.VMEM((nq, LANES), F32),
            pltpu.VMEM((n_blocks, nq, A_WIDTH), F32),
        ],
    )
    return pl.pallas_call(
        functools.partial(_moba_sample_kernel, steps=steps, n_blocks=n_blocks),
        grid_spec=grid_spec,
        out_shape=jax.ShapeDtypeStruct((nseq * steps, A_WIDTH), F32),
        compiler_params=pltpu.CompilerParams(
            dimension_semantics=("arbitrary", "arbitrary"), vmem_limit_bytes=VMEM_LIMIT),
        name="moba_sample",
    )(page_table.reshape(-1), q, k_new, v_new, cache_k, cache_k, cache_v, cache_v)


def _merge_kernel(x_ref, oa_ref, ob_ref, go_ref, ma_ref, mb_ref, gn_ref, wa_ref, wb_ref, wo_ref, h_ref):
    ya = _dot(oa_ref[...].astype(BF16), wa_ref[...])
    ob = ob_ref[...]
    normed = []
    for h in range(G_HEADS):
        seg = ob[:, h * G_DV:(h + 1) * G_DV]
        normed.append(seg * lax.rsqrt(jnp.mean(seg * seg, axis=-1, keepdims=True) + EPS))
    go = go_ref[...]
    ob = jnp.concatenate(normed, axis=1) * gn_ref[...] * (go * jax.nn.sigmoid(go))
    yb = _dot(ob.astype(BF16), wb_ref[...])
    merged = jax.nn.sigmoid(ma_ref[...]) * ya + jax.nn.sigmoid(mb_ref[...]) * yb
    h_ref[...] = x_ref[...] + _dot(merged.astype(BF16), wo_ref[...])


def _merge(x, o_a, o_b, gout, ma, mb, p):
    n = x.shape[0]
    tm = TOKEN_TILE
    row = lambda i: (i, 0)
    const = lambda i: (0, 0)
    return pl.pallas_call(
        _merge_kernel,
        grid=(n // tm,),
        in_specs=[
            pl.BlockSpec((tm, D_MODEL), row),
            pl.BlockSpec((tm, A_WIDTH), row),
            pl.BlockSpec((tm, G_VAL_WIDTH), row),
            pl.BlockSpec((tm, G_VAL_WIDTH), row),
            pl.BlockSpec((tm, D_MODEL), row),
            pl.BlockSpec((tm, D_MODEL), row),
            pl.BlockSpec((1, G_VAL_WIDTH), const),
            pl.BlockSpec((A_WIDTH, D_MODEL), const),
            pl.BlockSpec((G_VAL_WIDTH, D_MODEL), const),
            pl.BlockSpec((D_MODEL, D_MODEL), const),
        ],
        out_specs=pl.BlockSpec((tm, D_MODEL), row),
        out_shape=jax.ShapeDtypeStruct((n, D_MODEL), F32),
        compiler_params=pltpu.CompilerParams(dimension_semantics=("arbitrary",), vmem_limit_bytes=VMEM_LIMIT),
        name="merge",
    )(x, o_a, o_b, gout, ma, mb, p["gn"], p["wa"], p["wb"], p["wo"])


PEER_TOKEN_TILE = 256
PEER_KEYS_PER_STEP = 8
PEER_EXPERT_TILE = PEER_KEYS_PER_STEP * P_N_KEYS
BIG = 1e30


def _top_rows(s, count):
    cur = s
    vals = []
    for _ in range(count):
        m = jnp.max(cur, axis=0, keepdims=True)
        vals.append(m)
        cur = jnp.where(cur >= m, -jnp.inf, cur)
    return vals


def _peer_kernel(h_ref, g2_ref, wq_ref, sk1_ref, sk2_ref, u_ref, vt_ref, o_ref,
                 hb_ref, th_ref, a_ref, s2_ref, c_ref, acc_ref):
    e = pl.program_id(1)
    tm = h_ref.shape[0]

    @pl.when(e == 0)
    def _():
        hx = h_ref[...]
        hn = hx * lax.rsqrt(jnp.mean(hx * hx, axis=-1, keepdims=True) + EPS) * g2_ref[...]
        hb = hn.astype(BF16)
        hb_ref[...] = hb
        for hd in range(P_HEADS):
            qt = _nt(wq_ref[hd * 2 * P_HALF:(hd + 1) * 2 * P_HALF, :], hb)
            s1 = _dot(sk1_ref[hd], qt[:P_HALF].astype(BF16))
            s2 = _dot(sk2_ref[hd], qt[P_HALF:].astype(BF16))
            v1 = _top_rows(s1, P_TOPK)
            v2 = _top_rows(s2, P_TOPK)
            cand = [v1[a] + v2[b] for a in range(P_TOPK) for b in range(P_TOPK // (a + 1))]
            pad = -len(cand) % 8
            cand = jnp.concatenate(cand + [jnp.full((pad, tm), -jnp.inf, F32)], axis=0)
            tops = _top_rows(cand, P_TOPK + 1)
            cut = 0.5 * (tops[P_TOPK - 1] + tops[P_TOPK])
            z = jnp.sum(jnp.where(cand >= tops[P_TOPK - 1], jnp.exp(cand - tops[0]), 0.0),
                        axis=0, keepdims=True)
            th_ref[hd] = jnp.where(s1 >= v1[P_TOPK - 1], cut - s1, BIG)
            a_ref[hd] = jnp.exp(s1 - v1[0]) / z
            s2_ref[hd] = jnp.where(s2 >= v2[P_TOPK - 1], s2, -BIG)
            c_ref[hd] = jnp.exp(s2 - v2[0])
        acc_ref[...] = jnp.zeros_like(acc_ref)

    act = _nt(u_ref[...], hb_ref[...])
    gl = 0.5 * act * (1.0 + lax.erf(act * (1.0 / math.sqrt(2.0))))
    parts = []
    for ii in range(PEER_KEYS_PER_STEP):
        i = e * PEER_KEYS_PER_STEP + ii
        g = jnp.zeros((P_N_KEYS, tm), F32)
        for hd in range(P_HEADS):
            th = th_ref[hd, pl.ds(i, 1), :]
            aa = a_ref[hd, pl.ds(i, 1), :]
            g = g + jnp.where(s2_ref[hd] >= th, c_ref[hd], 0.0) * aa
        parts.append((g * gl[ii * P_N_KEYS:(ii + 1) * P_N_KEYS, :]).astype(BF16))
    acc_ref[...] += _dot(vt_ref[...], jnp.concatenate(parts, axis=0))

    @pl.when(e == pl.num_programs(1) - 1)
    def _():
        o_ref[...] = h_ref[...] + acc_ref[...].T


def _peer(h, p):
    n = h.shape[0]
    tm = PEER_TOKEN_TILE
    te = PEER_EXPERT_TILE
    n_exp = p["u"].shape[0]
    row = lambda t, e: (t, 0)
    const2 = lambda t, e: (0, 0)
    const3 = lambda t, e: (0, 0, 0)
    sel_scratch = pltpu.VMEM((P_HEADS, P_N_KEYS, tm), F32)
    return pl.pallas_call(
        _peer_kernel,
        grid=(n // tm, n_exp // te),
        in_specs=[
            pl.BlockSpec((tm, D_MODEL), row),
            pl.BlockSpec((1, D_MODEL), const2),
            pl.BlockSpec((P_HEADS * 2 * P_HALF, D_MODEL), const2),
            pl.BlockSpec((P_HEADS, P_N_KEYS, P_HALF), const3),
            pl.BlockSpec((P_HEADS, P_N_KEYS, P_HALF), const3),
            pl.BlockSpec((te, D_MODEL), lambda t, e: (e, 0)),
            pl.BlockSpec((D_MODEL, te), lambda t, e: (0, e)),
        ],
        out_specs=pl.BlockSpec((tm, D_MODEL), row),
        out_shape=jax.ShapeDtypeStruct((n, D_MODEL), F32),
        scratch_shapes=[
            pltpu.VMEM((tm, D_MODEL), BF16),
            sel_scratch, sel_scratch, sel_scratch, sel_scratch,
            pltpu.VMEM((D_MODEL, tm), F32),
        ],
        compiler_params=pltpu.CompilerParams(
            dimension_semantics=("arbitrary", "arbitrary"), vmem_limit_bytes=VMEM_LIMIT),
        name="peer",
    )(h, p["g2"], p["wq_t"], p["sk1"], p["sk2"], p["u"], p["v_t"])


def _prep_post_params(g_out_norm, w_branch_a, w_branch_b, w_out, norm2_g, peer_wq, sk1, sk2, peer_u, peer_v):
    return {
        "gn": jnp.tile(g_out_norm, G_HEADS).reshape(1, G_VAL_WIDTH),
        "wa": w_branch_a.astype(BF16),
        "wb": w_branch_b.astype(BF16),
        "wo": w_out.astype(BF16),
        "g2": norm2_g.reshape(1, D_MODEL),
        "wq_t": peer_wq.T.astype(BF16),
        "sk1": sk1.astype(BF16),
        "sk2": sk2.astype(BF16),
        "u": peer_u.astype(BF16),
        "v_t": peer_v.T.astype(BF16),
    }


def kernel(x_prompt, x_sample, cache_k, cache_v, state_gla, page_table, norm1_g, w_in, a_q_norm, a_k_norm,
           g_gate_w2, g_gate_b, g_out_norm, w_branch_a, w_branch_b, w_out, norm2_g, peer_wq,
           peer_subkeys1, peer_subkeys2, peer_u, peer_v):
    batch, seq, d = x_prompt.shape
    nseq, steps, _ = x_sample.shape
    depth, n_phys = cache_k.shape[:2]
    assert depth == 1 and d == D_MODEL
    assert seq % MOBA_BLOCK == 0 and MOBA_BLOCK == TOKEN_TILE
    past_len = page_table.shape[1] * PAGE_SIZE
    assert past_len % MOBA_BLOCK == 0 and TOKEN_TILE % steps == 0 and (nseq * steps) % TOKEN_TILE == 0
    assert math.gcd(steps, G_CHUNK) == steps and nseq % GLA_SAMPLE_SEQS == 0

    mp = _prep_mixer_params(norm1_g[0], w_in[0], a_q_norm[0], a_k_norm[0], g_gate_w2[0], g_gate_b[0])
    pp = _prep_post_params(g_out_norm[0], w_branch_a[0], w_branch_b[0], w_out[0], norm2_g[0], peer_wq[0],
                           peer_subkeys1[0], peer_subkeys2[0], peer_u[0], peer_v[0])

    nb = seq // TOKEN_TILE
    cos_p, sin_p = _rope_tables(jnp.arange(seq, dtype=jnp.int32))
    xp = x_prompt.reshape(batch * seq, d)
    q, k, v, gq, gk, gv, lg, go, ma, mb, km = _inproj(xp, lambda i: i % nb, cos_p, sin_p, mp)
    o_a = _moba_prompt(q, k, v, km.reshape(batch * nb, A_WIDTH), batch, seq)
    o_b, st_p = _gla_prompt(gq, gk, gv, lg, batch, seq)
    y_p = _peer(_merge(xp, o_a, o_b, go, ma, mb, pp), pp)

    pos_s = past_len + jnp.arange(TOKEN_TILE, dtype=jnp.int32) % steps
    cos_s, sin_s = _rope_tables(pos_s)
    xs = x_sample.reshape(nseq * steps, d)
    qs, ks, vs, gqs, gks, gvs, lgs, gos, mas, mbs, _ = _inproj(xs, lambda i: 0, cos_s, sin_s, mp)
    o_as = _moba_sample(qs, ks, vs, cache_k.reshape(n_phys, PAGE_SIZE, A_WIDTH),
                        cache_v.reshape(n_phys, PAGE_SIZE, A_WIDTH), page_table, steps)
    o_bs, st_s = _gla_sample(gqs, gks, gvs, lgs, state_gla.reshape(nseq, G_HEADS, G_DK, G_DV), steps)
    y_s = _peer(_merge(xs, o_as, o_bs, gos, mas, mbs, pp), pp)

    kv_p = (1, batch, seq, A_HEADS, A_HEAD_DIM)
    kv_s = (1, nseq, steps, A_HEADS, A_HEAD_DIM)
    return (y_p.reshape(batch, seq, d), y_s.reshape(nseq, steps, d),
            k.reshape(kv_p), v.reshape(kv_p), st_p[None],
            ks.reshape(kv_s), vs.reshape(kv_s), st_s[None])
```

```python
import functools
import math

import jax
import jax.numpy as jnp
from jax import lax
from jax.experimental import pallas as pl
from jax.experimental.pallas import tpu as pltpu

F32 = jnp.float32
BF16 = jnp.bfloat16

D_MODEL = 1024
PAST_LEN = 8192
PAGE_SIZE = 128
A_HEADS = 8
A_HEAD_DIM = 64
A_WIDTH = A_HEADS * A_HEAD_DIM
MOBA_BLOCK = 256
MOBA_TOPK = 3
ROPE_THETA = 10000.0
G_HEADS = 4
G_DK = 64
G_DV = 128
G_KEY_WIDTH = G_HEADS * G_DK
G_VAL_WIDTH = G_HEADS * G_DV
G_GATE_RANK = 16
G_GATE_NORMALIZER = 16.0
G_CHUNK = 64
G_SUB = 16
P_HEADS = 8
P_N_KEYS = 128
P_HALF = 128
P_TOPK = 16
EPS = 1e-6

LANES = 128
VMEM_LIMIT = 56 * 1024 * 1024

TOKEN_TILE = 256
GLR_PAD = LANES

NEG = -1e30


def _nt(a, b):
    return lax.dot_general(a, b, (((1,), (1,)), ((), ())), preferred_element_type=F32)


def _tn(a, b):
    return lax.dot_general(a, b, (((0,), (0,)), ((), ())), preferred_element_type=F32)


def _dot(a, b):
    return jnp.dot(a, b, preferred_element_type=F32)


def _split2(x):
    hi = x.astype(BF16)
    lo = (x - hi.astype(F32)).astype(BF16)
    return hi, lo


def _split3(x):
    hi = x.astype(BF16)
    r = x - hi.astype(F32)
    mid = r.astype(BF16)
    lo = (r - mid.astype(F32)).astype(BF16)
    return hi, mid, lo


_SEG_Q, _SEG_K, _SEG_V = 0, 512, 1024
_SEG_GQ, _SEG_GK, _SEG_GV = 1536, 1792, 2048
_SEG_GLR = 2560
_SEG_GOUT = _SEG_GLR + GLR_PAD
_SEG_MA = _SEG_GOUT + 512
_SEG_MB = _SEG_MA + 1024
_W_IN_PADDED = _SEG_MB + 1024


def _inproj_kernel(x_ref, g1_ref, w_ref, cos_ref, sin_ref, qn_ref, kn_ref, bd_ref, w2_ref, gb_ref,
                   q_ref, k_ref, v_ref, gq_ref, gk_ref, gv_ref, lg_ref, go_ref, ma_ref, mb_ref, km_ref):
    x = x_ref[...]
    xn = x * lax.rsqrt(jnp.mean(x * x, axis=-1, keepdims=True) + EPS) * g1_ref[...]
    xb = xn.astype(BF16)

    def proj(off, width):
        return _dot(xb, w_ref[:, off:off + width])

    cos = cos_ref[...]
    sin = sin_ref[...]
    lane = lax.broadcasted_iota(jnp.int32, (1, A_WIDTH), 1)
    first_half = (lane % A_HEAD_DIM) < (A_HEAD_DIM // 2)
    bd = bd_ref[...]

    def qk_epilogue(z, gamma):
        hi, lo = _split2(z * z)
        ms = (_dot(hi, bd) + _dot(lo, bd)) * (1.0 / A_HEAD_DIM)
        y = z * lax.rsqrt(ms + EPS) * gamma
        rot = jnp.where(first_half,
                        -pltpu.roll(y, A_WIDTH - A_HEAD_DIM // 2, 1),
                        pltpu.roll(y, A_HEAD_DIM // 2, 1))
        return y * cos + rot * sin

    q_ref[...] = qk_epilogue(proj(_SEG_Q, A_WIDTH), qn_ref[...])
    k = qk_epilogue(proj(_SEG_K, A_WIDTH), kn_ref[...])
    k_ref[...] = k
    km_ref[0] = jnp.mean(k, axis=0, keepdims=True)
    v_ref[...] = proj(_SEG_V, A_WIDTH)
    gq_ref[...] = proj(_SEG_GQ, G_KEY_WIDTH) * (G_DK ** -0.5)
    gk_ref[...] = proj(_SEG_GK, G_KEY_WIDTH)
    gv_ref[...] = proj(_SEG_GV, G_VAL_WIDTH)
    glr = proj(_SEG_GLR, GLR_PAD)
    t = _dot(glr.astype(BF16), w2_ref[...]) + gb_ref[...]
    lg_ref[...] = (jnp.minimum(t, 0.0) - jnp.log1p(jnp.exp(-jnp.abs(t)))) * (1.0 / G_GATE_NORMALIZER)
    go_ref[...] = proj(_SEG_GOUT, G_VAL_WIDTH)
    ma_ref[...] = proj(_SEG_MA, D_MODEL)
    mb_ref[...] = proj(_SEG_MB, D_MODEL)


def _inproj(x, pos_block_of_tile, cos_tab, sin_tab, p):
    n = x.shape[0]
    tm = TOKEN_TILE
    nt = n // tm
    row = lambda i: (i, 0)
    const = lambda i: (0, 0)
    widths = (A_WIDTH, A_WIDTH, A_WIDTH, G_KEY_WIDTH, G_KEY_WIDTH, G_VAL_WIDTH, G_KEY_WIDTH,
              G_VAL_WIDTH, D_MODEL, D_MODEL)
    out_shape = [jax.ShapeDtypeStruct((n, w), F32) for w in widths]
    out_specs = [pl.BlockSpec((tm, w), row) for w in widths]
    out_shape.append(jax.ShapeDtypeStruct((nt, 1, A_WIDTH), F32))
    out_specs.append(pl.BlockSpec((1, 1, A_WIDTH), lambda i: (i, 0, 0)))
    tab_spec = pl.BlockSpec((tm, A_WIDTH), lambda i: (pos_block_of_tile(i), 0))
    return pl.pallas_call(
        _inproj_kernel,
        grid=(nt,),
        in_specs=[
            pl.BlockSpec((tm, D_MODEL), row),
            pl.BlockSpec((1, D_MODEL), const),
            pl.BlockSpec((D_MODEL, _W_IN_PADDED), const),
            tab_spec, tab_spec,
            pl.BlockSpec((1, A_WIDTH), const),
            pl.BlockSpec((1, A_WIDTH), const),
            pl.BlockSpec((A_WIDTH, A_WIDTH), const),
            pl.BlockSpec((GLR_PAD, G_KEY_WIDTH), const),
            pl.BlockSpec((1, G_KEY_WIDTH), const),
        ],
        out_specs=out_specs,
        out_shape=out_shape,
        compiler_params=pltpu.CompilerParams(dimension_semantics=("arbitrary",), vmem_limit_bytes=VMEM_LIMIT),
        name="inproj",
    )(x, p["g1"], p["w_in"], cos_tab, sin_tab, p["qn"], p["kn"], p["bd"], p["w2"], p["gb"])


def _rope_tables(pos):
    half = A_HEAD_DIM // 2
    inv_freq = ROPE_THETA ** (-jnp.arange(half, dtype=F32) / half)
    ang = pos.astype(F32)[:, None] * inv_freq[None, :]
    reps = A_WIDTH // half
    return jnp.tile(jnp.cos(ang), (1, reps)), jnp.tile(jnp.sin(ang), (1, reps))


def _prep_mixer_params(norm1_g, w_in, a_q_norm, a_k_norm, g_gate_w2, g_gate_b):
    glr_off = 3 * A_WIDTH + 2 * G_KEY_WIDTH + G_VAL_WIDTH
    w_pad = jnp.concatenate([
        w_in[:, :glr_off],
        jnp.pad(w_in[:, glr_off:glr_off + G_GATE_RANK], ((0, 0), (0, GLR_PAD - G_GATE_RANK))),
        w_in[:, glr_off + G_GATE_RANK:]], axis=1).astype(BF16)
    head_id = jnp.arange(A_WIDTH) // A_HEAD_DIM
    return {
        "g1": norm1_g.reshape(1, D_MODEL),
        "w_in": w_pad,
        "qn": jnp.tile(a_q_norm, A_HEADS).reshape(1, A_WIDTH),
        "kn": jnp.tile(a_k_norm, A_HEADS).reshape(1, A_WIDTH),
        "bd": (head_id[:, None] == head_id[None, :]).astype(BF16),
        "w2": jnp.pad(g_gate_w2, ((0, GLR_PAD - G_GATE_RANK), (0, 0))).astype(BF16),
        "gb": g_gate_b.reshape(1, G_KEY_WIDTH),
    }


def _topk_rows_mask(s, valid, n_sel):
    n = s.shape[0]
    sm = jnp.where(valid, s, -jnp.inf)
    ridx = lax.broadcasted_iota(jnp.int32, s.shape, 0)
    cnt = jnp.zeros(s.shape, jnp.int32)
    for r in range(n):
        row = sm[r:r + 1, :]
        beats = (row > sm) | ((row == sm) & (r < ridx))
        cnt = cnt + beats.astype(jnp.int32)
    return valid & (cnt < n_sel)


PROMPT_BLOCKS_PER_ITER = 4


def _moba_prompt_kernel(q_ref, k_ref, v_ref, km_ref, o_ref, sel_ref):
    ob = pl.program_id(2)
    blk = MOBA_BLOCK
    nb = km_ref.shape[0]
    scale = A_HEAD_DIM ** -0.5
    kpos = lax.broadcasted_iota(jnp.int32, (blk, blk), 0)
    qpos = lax.broadcasted_iota(jnp.int32, (blk, blk), 1)
    causal = kpos <= qpos
    n_heads = LANES // A_HEAD_DIM
    head_lanes = [slice(hh * A_HEAD_DIM, (hh + 1) * A_HEAD_DIM) for hh in range(n_heads)]
    nidx = lax.broadcasted_iota(jnp.int32, (nb, blk), 0)
    qs = []
    for hh, sl in enumerate(head_lanes):
        q = q_ref[:, sl]
        s_sel = lax.dot_general(km_ref[:, sl], q, (((1,), (1,)), ((), ())),
                                precision=lax.Precision.HIGHEST, preferred_element_type=F32)
        sel_ref[hh] = _topk_rows_mask(s_sel, nidx < ob, MOBA_TOPK).astype(F32)
        qs.append((q * scale).astype(BF16))

    def scores(n, hh):
        kb = k_ref[pl.ds(n * blk, blk), head_lanes[hh]].astype(BF16)
        return _nt(kb, qs[hh])

    def values_t(n, hh):
        return v_ref[pl.ds(n * blk, blk), head_lanes[hh]].T.astype(BF16)

    init = []
    for hh in range(n_heads):
        s0 = jnp.where(causal, scores(ob, hh), -jnp.inf)
        m0 = jnp.max(s0, axis=0, keepdims=True)
        p0 = jnp.exp(s0 - m0)
        init.append((m0, jnp.sum(p0, axis=0, keepdims=True), _dot(values_t(ob, hh), p0.astype(BF16))))

    def body(it, carry):
        new = []
        for hh in range(n_heads):
            m, l, acc = carry[hh]
            blocks = [jnp.minimum(it * PROMPT_BLOCKS_PER_ITER + j, nb - 1)
                      for j in range(PROMPT_BLOCKS_PER_ITER)]
            ss = [jnp.where(sel_ref[hh, pl.ds(n, 1), :] > 0.5, scores(n, hh), -jnp.inf) for n in blocks]
            m_new = m
            for s in ss:
                m_new = jnp.maximum(m_new, jnp.max(s, axis=0, keepdims=True))
            alpha = jnp.exp(m - m_new)
            l = alpha * l
            acc = alpha * acc
            for n, s in zip(blocks, ss):
                p = jnp.exp(s - m_new)
                l = l + jnp.sum(p, axis=0, keepdims=True)
                acc = acc + _dot(values_t(n, hh), p.astype(BF16))
            new.append((m_new, l, acc))
        return tuple(new)

    n_iter = (ob + PROMPT_BLOCKS_PER_ITER - 1) // PROMPT_BLOCKS_PER_ITER
    final = lax.fori_loop(0, n_iter, body, tuple(init))
    o_ref[...] = jnp.concatenate([(acc / l).T for _, l, acc in final], axis=1)


def _moba_prompt(q, k, v, kmeans, batch, seq):
    nb = seq // MOBA_BLOCK
    hp = A_WIDTH // LANES
    return pl.pallas_call(
        _moba_prompt_kernel,
        grid=(batch, hp, nb),
        in_specs=[
            pl.BlockSpec((MOBA_BLOCK, LANES), lambda b, h, i: (b * nb + i, h)),
            pl.BlockSpec((seq, LANES), lambda b, h, i: (b, h)),
            pl.BlockSpec((seq, LANES), lambda b, h, i: (b, h)),
            pl.BlockSpec((nb, LANES), lambda b, h, i: (b, h)),
        ],
        out_specs=pl.BlockSpec((MOBA_BLOCK, LANES), lambda b, h, i: (b * nb + i, h)),
        out_shape=jax.ShapeDtypeStruct((batch * seq, A_WIDTH), F32),
        scratch_shapes=[pltpu.VMEM((LANES // A_HEAD_DIM, nb, MOBA_BLOCK), F32)],
        compiler_params=pltpu.CompilerParams(
            dimension_semantics=("arbitrary", "arbitrary", "arbitrary"), vmem_limit_bytes=VMEM_LIMIT),
        name="moba_prompt",
    )(q, k, v, kmeans)


def _cumsum_rows(tri, g):
    hi, mid, lo = _split3(g)
    return _dot(tri, hi) + _dot(tri, mid) + _dot(tri, lo)


def _gla_prompt_kernel(q_ref, k_ref, v_ref, g_ref, tri_ref, o_ref, s_ref, st_ref):
    t = pl.program_id(1)
    c, sub = G_CHUNK, G_SUB

    @pl.when(t == 0)
    def _():
        st_ref[...] = jnp.zeros_like(st_ref)

    tri = tri_ref[...]
    for ci in range(TOKEN_TILE // c):
        rows = slice(ci * c, (ci + 1) * c)
        q, k, v, g = q_ref[rows, :], k_ref[rows, :], v_ref[rows, :], g_ref[rows, :]
        b = _cumsum_rows(tri, g)
        blast = b[c - 1:c, :]
        q_inter = (q * jnp.exp(b)).astype(BF16)
        k_dec = (k * jnp.exp(blast - b)).astype(BF16)
        vb = v.astype(BF16)
        q_sub, k_sub = [], []
        for i in range(c // sub):
            bref = b[i * sub - 1:i * sub, :] if i > 0 else jnp.zeros((1, G_KEY_WIDTH), F32)
            r = slice(i * sub, (i + 1) * sub)
            ncol = (i + 1) * sub
            q_sub.append((q[r, :] * jnp.exp(b[r, :] - bref)).astype(BF16))
            k_sub.append((k[:ncol, :] * jnp.exp(bref - b[:ncol, :])).astype(BF16))
        decay = jnp.exp(blast)
        outs = []
        for h in range(G_HEADS):
            hs = slice(h * G_DK, (h + 1) * G_DK)
            vs = slice(h * G_DV, (h + 1) * G_DV)
            st = st_ref[h]
            o = _nt(q_inter[:, hs], st.astype(BF16))
            intra = []
            for i in range(c // sub):
                ncol = (i + 1) * sub
                a = _nt(q_sub[i][:, hs], k_sub[i][:, hs])
                rr = lax.broadcasted_iota(jnp.int32, (sub, ncol), 0) + i * sub
                cc = lax.broadcasted_iota(jnp.int32, (sub, ncol), 1)
                a = jnp.where(cc <= rr, a, 0.0).astype(BF16)
                intra.append(_dot(a, vb[:ncol, vs]))
            outs.append(o + jnp.concatenate(intra, axis=0))
            st_ref[h] = st * decay[:, hs] + _tn(vb[:, vs], k_dec[:, hs])
        o_ref[rows, :] = jnp.concatenate(outs, axis=1)

    @pl.when(t == pl.num_programs(1) - 1)
    def _():
        for h in range(G_HEADS):
            s_ref[0, h] = st_ref[h].T


def _gla_prompt(gq, gk, gv, logg, batch, seq):
    nt = seq // TOKEN_TILE
    row = lambda b, t: (b * nt + t, 0)
    tri = jnp.tril(jnp.ones((G_CHUNK, G_CHUNK), F32)).astype(BF16)
    return pl.pallas_call(
        _gla_prompt_kernel,
        grid=(batch, nt),
        in_specs=[
            pl.BlockSpec((TOKEN_TILE, G_KEY_WIDTH), row),
            pl.BlockSpec((TOKEN_TILE, G_KEY_WIDTH), row),
            pl.BlockSpec((TOKEN_TILE, G_VAL_WIDTH), row),
            pl.BlockSpec((TOKEN_TILE, G_KEY_WIDTH), row),
            pl.BlockSpec((G_CHUNK, G_CHUNK), lambda b, t: (0, 0)),
        ],
        out_specs=[
            pl.BlockSpec((TOKEN_TILE, G_VAL_WIDTH), row),
            pl.BlockSpec((1, G_HEADS, G_DK, G_DV), lambda b, t: (b, 0, 0, 0)),
        ],
        out_shape=[
            jax.ShapeDtypeStruct((batch * seq, G_VAL_WIDTH), F32),
            jax.ShapeDtypeStruct((batch, G_HEADS, G_DK, G_DV), F32),
        ],
        scratch_shapes=[pltpu.VMEM((G_HEADS, G_DV, G_DK), F32)],
        compiler_params=pltpu.CompilerParams(
            dimension_semantics=("arbitrary", "arbitrary"), vmem_limit_bytes=VMEM_LIMIT),
        name="gla_prompt",
    )(gq, gk, gv, logg, tri)


GLA_SAMPLE_SEQS = 8


def _gla_sample_kernel(q_ref, k_ref, v_ref, g_ref, tri_ref, s0_ref, o_ref, s_ref, *, steps):
    q, k, v, g = q_ref[...], k_ref[...], v_ref[...], g_ref[...]
    b = _cumsum_rows(tri_ref[...], g)
    eb = jnp.exp(b)
    qd = (q * eb).astype(BF16)
    kinv = (k * jnp.exp(-b)).astype(BF16)
    vb = v.astype(BF16)
    rr = lax.broadcasted_iota(jnp.int32, (steps, steps), 0)
    cc = lax.broadcasted_iota(jnp.int32, (steps, steps), 1)
    eye = lax.broadcasted_iota(jnp.int32, (G_DK, G_DK), 0) == lax.broadcasted_iota(jnp.int32, (G_DK, G_DK), 1)
    out_rows = []
    for s in range(GLA_SAMPLE_SEQS):
        rows = slice(s * steps, (s + 1) * steps)
        blast = b[(s + 1) * steps - 1:(s + 1) * steps, :]
        k_dec = (k[rows, :] * jnp.exp(blast - b[rows, :])).astype(BF16)
        outs = []
        for h in range(G_HEADS):
            hs = slice(h * G_DK, (h + 1) * G_DK)
            vs = slice(h * G_DV, (h + 1) * G_DV)
            s0 = s0_ref[s, h]
            a = jnp.where(cc <= rr, _nt(qd[rows, hs], kinv[rows, hs]), 0.0).astype(BF16)
            outs.append(_dot(qd[rows, hs], s0.astype(BF16)) + _dot(a, vb[rows, vs]))
            dcol = jnp.sum(jnp.where(eye, jnp.exp(blast[:, hs]), 0.0), axis=1, keepdims=True)
            s_ref[s, h] = dcol * s0 + _tn(k_dec[:, hs], vb[rows, vs])
        out_rows.append(jnp.concatenate(outs, axis=1))
    o_ref[...] = jnp.concatenate(out_rows, axis=0)


def _gla_sample(gq, gk, gv, logg, state, steps):
    nseq = state.shape[0]
    sb = GLA_SAMPLE_SEQS
    rows = sb * steps
    idx = jnp.arange(rows)
    tri = ((idx[:, None] >= idx[None, :]) & (idx[:, None] // steps == idx[None, :] // steps)).astype(BF16)
    row = lambda i: (i, 0)
    return pl.pallas_call(
        functools.partial(_gla_sample_kernel, steps=steps),
        grid=(nseq // sb,),
        in_specs=[
            pl.BlockSpec((rows, G_KEY_WIDTH), row),
            pl.BlockSpec((rows, G_KEY_WIDTH), row),
            pl.BlockSpec((rows, G_VAL_WIDTH), row),
            pl.BlockSpec((rows, G_KEY_WIDTH), row),
            pl.BlockSpec((rows, rows), lambda i: (0, 0)),
            pl.BlockSpec((sb, G_HEADS, G_DK, G_DV), lambda i: (i, 0, 0, 0)),
        ],
        out_specs=[
            pl.BlockSpec((rows, G_VAL_WIDTH), row),
            pl.BlockSpec((sb, G_HEADS, G_DK, G_DV), lambda i: (i, 0, 0, 0)),
        ],
        out_shape=[
            jax.ShapeDtypeStruct((nseq * steps, G_VAL_WIDTH), F32),
            jax.ShapeDtypeStruct(state.shape, F32),
        ],
        compiler_params=pltpu.CompilerParams(dimension_semantics=("arbitrary",), vmem_limit_bytes=VMEM_LIMIT),
        name="gla_sample",
    )(gq, gk, gv, logg, tri, state)


PAGES_PER_BLOCK = MOBA_BLOCK // PAGE_SIZE


SAMPLE_PAGES_PER_STEP = 16


def _moba_sample_kernel(pt_ref, q_ref, kn_ref, vn_ref, *refs, steps, n_blocks):
    del pt_ref
    pps = SAMPLE_PAGES_PER_STEP
    k_refs, v_refs = refs[:pps], refs[pps:2 * pps]
    o_ref, wq_ref, m_ref, l_ref, ss_ref, acc_ref = refs[2 * pps:]
    g = pl.program_id(1)
    nq = A_HEADS * steps
    scale = A_HEAD_DIM ** -0.5

    @pl.when(g == 0)
    def _():
        qt = jnp.concatenate([q_ref[...]] * A_HEADS, axis=0)
        rh = lax.broadcasted_iota(jnp.int32, (nq, A_WIDTH), 0) // steps
        ch = lax.broadcasted_iota(jnp.int32, (nq, A_WIDTH), 1) // A_HEAD_DIM
        wq_ref[...] = jnp.where(rh == ch, qt * scale, 0.0).astype(BF16)

    wq = wq_ref[...]
    for bi in range(pps // PAGES_PER_BLOCK):
        n = g * (pps // PAGES_PER_BLOCK) + bi
        pages = range(bi * PAGES_PER_BLOCK, (bi + 1) * PAGES_PER_BLOCK)
        st = jnp.concatenate([_dot(wq, k_refs[j][0].astype(BF16)).T for j in pages], axis=0)
        m = jnp.max(st, axis=0, keepdims=True)
        p = jnp.exp(st - m)
        m_ref[pl.ds(n, 1), :] = m
        l_ref[pl.ds(n, 1), :] = jnp.sum(p, axis=0, keepdims=True)
        ss_ref[pl.ds(n, 1), :] = jnp.sum(st, axis=0, keepdims=True)
        pb = p.astype(BF16)
        acc = None
        for jj, j in enumerate(pages):
            part = _dot(v_refs[j][0].astype(BF16), pb[jj * PAGE_SIZE:(jj + 1) * PAGE_SIZE, :])
            acc = part if acc is None else acc + part
        acc_ref[n] = acc

    @pl.when(g == pl.num_programs(1) - 1)
    def _():
        ss = ss_ref[...]
        sel = _topk_rows_mask(ss, jnp.ones(ss.shape, jnp.bool_), min(MOBA_TOPK, n_blocks))
        s_own = _nt(kn_ref[...].astype(BF16), wq)
        tk = lax.broadcasted_iota(jnp.int32, (steps, nq), 0)
        tq = lax.broadcasted_iota(jnp.int32, (steps, nq), 1) % steps
        s_own = jnp.where(tk <= tq, s_own, -jnp.inf)
        mm = jnp.where(sel, m_ref[...], -jnp.inf)
        big = jnp.maximum(jnp.max(mm, axis=0, keepdims=True), jnp.max(s_own, axis=0, keepdims=True))
        w = jnp.where(sel, jnp.exp(mm - big), 0.0)
        p_own = jnp.exp(s_own - big)
        denom = jnp.sum(w * l_ref[...], axis=0, keepdims=True) + jnp.sum(p_own, axis=0, keepdims=True)
        acc = _tn(vn_ref[...].astype(BF16), p_own.astype(BF16))
        for j in range(n_blocks):
            acc = acc + w[j:j + 1, :] * acc_ref[j]
        out = (acc / denom).T
        o_ref[...] = jnp.concatenate(
            [out[h * steps:(h + 1) * steps, h * A_HEAD_DIM:(h + 1) * A_HEAD_DIM] for h in range(A_HEADS)], axis=1)


def _moba_sample(q, k_new, v_new, cache_kt, cache_vt, page_table, steps):
    nseq, n_pages = page_table.shape
    n_blocks = n_pages // PAGES_PER_BLOCK
    pps = SAMPLE_PAGES_PER_STEP
    nq = A_HEADS * steps
    tok = pl.BlockSpec((steps, A_WIDTH), lambda s, g, pt: (s, 0))

    def page(j):
        return pl.BlockSpec((1, A_WIDTH, PAGE_SIZE), lambda s, g, pt: (pt[s * n_pages + g * pps + j], 0, 0))

    pages = [page(j) for j in range(pps)]
    grid_spec = pltpu.PrefetchScalarGridSpec(
        num_scalar_prefetch=1,
        grid=(nseq, n_pages // pps),
        in_specs=[tok, tok, tok] + pages + pages,
        out_specs=tok,
        scratch_shapes=[
            pltpu.VMEM((nq, A_WIDTH), BF16),
            pltpu.VMEM((n_blocks, nq), F32),
            pltpu.VMEM((n_blocks, nq), F32),
            pltpu.VMEM((n_blocks, nq), F32),
            pltpu.VMEM((n_blocks, A_WIDTH, nq), F32),
        ],
    )
    return pl.pallas_call(
        functools.partial(_moba_sample_kernel, steps=steps, n_blocks=n_blocks),
        grid_spec=grid_spec,
        out_shape=jax.ShapeDtypeStruct((nseq * steps, A_WIDTH), F32),
        compiler_params=pltpu.CompilerParams(
            dimension_semantics=("arbitrary", "arbitrary"), vmem_limit_bytes=VMEM_LIMIT),
        name="moba_sample",
    )(page_table.reshape(-1), q, k_new, v_new, *([cache_kt] * pps), *([cache_vt] * pps))


def _merge_kernel(x_ref, oa_ref, ob_ref, go_ref, ma_ref, mb_ref, gn_ref, wa_ref, wb_ref, wo_ref, h_ref):
    ya = _dot(oa_ref[...].astype(BF16), wa_ref[...])
    ob = ob_ref[...]
    normed = []
    for h in range(G_HEADS):
        seg = ob[:, h * G_DV:(h + 1) * G_DV]
        normed.append(seg * lax.rsqrt(jnp.mean(seg * seg, axis=-1, keepdims=True) + EPS))
    go = go_ref[...]
    ob = jnp.concatenate(normed, axis=1) * gn_ref[...] * (go * jax.nn.sigmoid(go))
    yb = _dot(ob.astype(BF16), wb_ref[...])
    merged = jax.nn.sigmoid(ma_ref[...]) * ya + jax.nn.sigmoid(mb_ref[...]) * yb
    h_ref[...] = x_ref[...] + _dot(merged.astype(BF16), wo_ref[...])


def _merge(x, o_a, o_b, gout, ma, mb, p):
    n = x.shape[0]
    tm = TOKEN_TILE
    row = lambda i: (i, 0)
    const = lambda i: (0, 0)
    return pl.pallas_call(
        _merge_kernel,
        grid=(n // tm,),
        in_specs=[
            pl.BlockSpec((tm, D_MODEL), row),
            pl.BlockSpec((tm, A_WIDTH), row),
            pl.BlockSpec((tm, G_VAL_WIDTH), row),
            pl.BlockSpec((tm, G_VAL_WIDTH), row),
            pl.BlockSpec((tm, D_MODEL), row),
            pl.BlockSpec((tm, D_MODEL), row),
            pl.BlockSpec((1, G_VAL_WIDTH), const),
            pl.BlockSpec((A_WIDTH, D_MODEL), const),
            pl.BlockSpec((G_VAL_WIDTH, D_MODEL), const),
            pl.BlockSpec((D_MODEL, D_MODEL), const),
        ],
        out_specs=pl.BlockSpec((tm, D_MODEL), row),
        out_shape=jax.ShapeDtypeStruct((n, D_MODEL), F32),
        compiler_params=pltpu.CompilerParams(dimension_semantics=("arbitrary",), vmem_limit_bytes=VMEM_LIMIT),
        name="merge",
    )(x, o_a, o_b, gout, ma, mb, p["gn"], p["wa"], p["wb"], p["wo"])


PEER_TOKEN_TILE = 512
PEER_KEYS_PER_STEP = 8
PEER_EXPERT_TILE = PEER_KEYS_PER_STEP * P_N_KEYS
PEER_GATE_GROUP = 4
PEER_GATE_KEY_SPLIT = 2
BIG = 1e30


def _top_rows(s, count):
    cur = s
    vals = []
    for _ in range(count):
        m = jnp.max(cur, axis=0, keepdims=True)
        vals.append(m)
        cur = jnp.where(cur >= m, -jnp.inf, cur)
    return vals


def _peer_kernel(h_ref, g2_ref, wq_ref, sk1_ref, sk2_ref, u_ref, vt_ref, o_ref,
                 hb_ref, th_ref, a_ref, s2_ref, c_ref, acc_ref, act_ref, p_ref):
    e = pl.program_id(1)
    tm = h_ref.shape[0]

    @pl.when(e == 0)
    def _():
        hx = h_ref[...]
        hn = hx * lax.rsqrt(jnp.mean(hx * hx, axis=-1, keepdims=True) + EPS) * g2_ref[...]
        hb = hn.astype(BF16)
        hb_ref[...] = hb
        for hd in range(P_HEADS):
            qt = _nt(wq_ref[hd * 2 * P_HALF:(hd + 1) * 2 * P_HALF, :], hb)
            s1 = _dot(sk1_ref[hd], qt[:P_HALF].astype(BF16))
            s2 = _dot(sk2_ref[hd], qt[P_HALF:].astype(BF16))
            v1 = _top_rows(s1, P_TOPK)
            v2 = _top_rows(s2, P_TOPK)
            cand = [v1[a] + v2[b] for a in range(P_TOPK) for b in range(P_TOPK // (a + 1))]
            pad = -len(cand) % 8
            cand = jnp.concatenate(cand + [jnp.full((pad, tm), -jnp.inf, F32)], axis=0)
            tops = _top_rows(cand, P_TOPK + 1)
            cut = 0.5 * (tops[P_TOPK - 1] + tops[P_TOPK])
            z = jnp.sum(jnp.where(cand >= tops[P_TOPK - 1], jnp.exp(cand - tops[0]), 0.0),
                        axis=0, keepdims=True)
            th = jnp.where(s1 >= v1[P_TOPK - 1], cut - s1, BIG)
            aa = jnp.exp(s1 - v1[0]) / z
            s2m = jnp.where(s2 >= v2[P_TOPK - 1], s2, -BIG)
            cc2 = jnp.exp(s2 - v2[0])
            for cc in range(tm // LANES):
                cols = slice(cc * LANES, (cc + 1) * LANES)
                th_ref[hd, cc] = th[:, cols]
                a_ref[hd, cc] = aa[:, cols]
                s2_ref[hd, cc] = s2m[:, cols]
                c_ref[hd, cc] = cc2[:, cols]
        acc_ref[...] = jnp.zeros_like(acc_ref)

    n_cc = tm // LANES
    group = PEER_GATE_GROUP
    half = P_N_KEYS // PEER_GATE_KEY_SPLIT

    def gate_unit(grp, cc, base, act_r, p_w):
        cols = slice(cc * LANES, (cc + 1) * LANES)
        for jh in range(PEER_GATE_KEY_SPLIT):
            jr = slice(jh * half, (jh + 1) * half)
            g = [jnp.zeros((half, LANES), F32) for _ in range(group)]
            for hd in range(P_HEADS):
                s2h = s2_ref[hd, cc, jr, :]
                ch = c_ref[hd, cc, jr, :]
                for k in range(group):
                    i = base + grp * group + k
                    th = th_ref[hd, cc, pl.ds(i, 1), :]
                    aa = a_ref[hd, cc, pl.ds(i, 1), :]
                    g[k] = g[k] + jnp.where(s2h >= th, ch, 0.0) * aa
            for k in range(group):
                r0 = (grp * group + k) * P_N_KEYS + jh * half
                act = act_r[r0:r0 + half, cols]
                gl = 0.5 * act * (1.0 + lax.erf(act * (1.0 / math.sqrt(2.0))))
                p_w[r0:r0 + half, cols] = (g[k] * gl).astype(BF16)

    act_ref[...] = _nt(u_ref[...], hb_ref[...])
    for grp in range(PEER_KEYS_PER_STEP // group):
        for cc in range(n_cc):
            gate_unit(grp, cc, e * PEER_KEYS_PER_STEP, act_ref, p_ref)
    acc_ref[...] += _dot(vt_ref[...], p_ref[...])

    @pl.when(e == pl.num_programs(1) - 1)
    def _():
        o_ref[...] = h_ref[...] + acc_ref[...].T


def _peer(h, p):
    n = h.shape[0]
    tm = PEER_TOKEN_TILE
    te = PEER_EXPERT_TILE
    n_et = p["u"].shape[0] // te
    row = lambda t, e: (t, 0)
    const2 = lambda t, e: (0, 0)
    const3 = lambda t, e: (0, 0, 0)
    sel_scratch = pltpu.VMEM((P_HEADS, tm // LANES, P_N_KEYS, LANES), F32)
    return pl.pallas_call(
        _peer_kernel,
        grid=(n // tm, n_et),
        in_specs=[
            pl.BlockSpec((tm, D_MODEL), row),
            pl.BlockSpec((1, D_MODEL), const2),
            pl.BlockSpec((P_HEADS * 2 * P_HALF, D_MODEL), const2),
            pl.BlockSpec((P_HEADS, P_N_KEYS, P_HALF), const3),
            pl.BlockSpec((P_HEADS, P_N_KEYS, P_HALF), const3),
            pl.BlockSpec((te, D_MODEL), lambda t, e: (e, 0)),
            pl.BlockSpec((D_MODEL, te), lambda t, e: (0, e)),
        ],
        out_specs=pl.BlockSpec((tm, D_MODEL), row),
        out_shape=jax.ShapeDtypeStruct((n, D_MODEL), F32),
        scratch_shapes=[
            pltpu.VMEM((tm, D_MODEL), BF16),
            sel_scratch, sel_scratch, sel_scratch, sel_scratch,
            pltpu.VMEM((D_MODEL, tm), F32),
            pltpu.VMEM((te, tm), F32),
            pltpu.VMEM((te, tm), BF16),
        ],
        compiler_params=pltpu.CompilerParams(
            dimension_semantics=("arbitrary", "arbitrary"), vmem_limit_bytes=VMEM_LIMIT),
        name="peer",
    )(h, p["g2"], p["wq_t"], p["sk1"], p["sk2"], p["u"], p["v_t"])


def _prep_post_params(g_out_norm, w_branch_a, w_branch_b, w_out, norm2_g, peer_wq, sk1, sk2, peer_u, peer_v):
    return {
        "gn": jnp.tile(g_out_norm, G_HEADS).reshape(1, G_VAL_WIDTH),
        "wa": w_branch_a.astype(BF16),
        "wb": w_branch_b.astype(BF16),
        "wo": w_out.astype(BF16),
        "g2": norm2_g.reshape(1, D_MODEL),
        "wq_t": peer_wq.T.astype(BF16),
        "sk1": sk1.astype(BF16),
        "sk2": sk2.astype(BF16),
        "u": peer_u.astype(BF16),
        "v_t": peer_v.T.astype(BF16),
    }


def kernel(x_prompt, x_sample, cache_k, cache_v, state_gla, page_table, norm1_g, w_in, a_q_norm, a_k_norm,
           g_gate_w2, g_gate_b, g_out_norm, w_branch_a, w_branch_b, w_out, norm2_g, peer_wq,
           peer_subkeys1, peer_subkeys2, peer_u, peer_v):
    batch, seq, d = x_prompt.shape
    nseq, steps, _ = x_sample.shape
    depth, n_phys = cache_k.shape[:2]
    assert depth == 1 and d == D_MODEL
    assert seq % MOBA_BLOCK == 0 and MOBA_BLOCK == TOKEN_TILE
    past_len = page_table.shape[1] * PAGE_SIZE
    assert past_len % MOBA_BLOCK == 0 and TOKEN_TILE % steps == 0 and (nseq * steps) % TOKEN_TILE == 0
    assert math.gcd(steps, G_CHUNK) == steps and nseq % GLA_SAMPLE_SEQS == 0

    mp = _prep_mixer_params(norm1_g[0], w_in[0], a_q_norm[0], a_k_norm[0], g_gate_w2[0], g_gate_b[0])
    pp = _prep_post_params(g_out_norm[0], w_branch_a[0], w_branch_b[0], w_out[0], norm2_g[0], peer_wq[0],
                           peer_subkeys1[0], peer_subkeys2[0], peer_u[0], peer_v[0])

    nb = seq // TOKEN_TILE
    cos_p, sin_p = _rope_tables(jnp.arange(seq, dtype=jnp.int32))
    xp = x_prompt.reshape(batch * seq, d)
    q, k, v, gq, gk, gv, lg, go, ma, mb, km = _inproj(xp, lambda i: i % nb, cos_p, sin_p, mp)
    o_a = _moba_prompt(q, k, v, km.reshape(batch * nb, A_WIDTH), batch, seq)
    o_b, st_p = _gla_prompt(gq, gk, gv, lg, batch, seq)
    y_p = _peer(_merge(xp, o_a, o_b, go, ma, mb, pp), pp)

    pos_s = past_len + jnp.arange(TOKEN_TILE, dtype=jnp.int32) % steps
    cos_s, sin_s = _rope_tables(pos_s)
    xs = x_sample.reshape(nseq * steps, d)
    qs, ks, vs, gqs, gks, gvs, lgs, gos, mas, mbs, _ = _inproj(xs, lambda i: 0, cos_s, sin_s, mp)
    def pages_t(c):
        c = c.reshape(n_phys, PAGE_SIZE, A_HEADS, A_HEAD_DIM)
        return jnp.transpose(c, (0, 2, 3, 1)).reshape(n_phys, A_WIDTH, PAGE_SIZE)

    o_as = _moba_sample(qs, ks, vs, pages_t(cache_k), pages_t(cache_v), page_table, steps)
    o_bs, st_s = _gla_sample(gqs, gks, gvs, lgs, state_gla.reshape(nseq, G_HEADS, G_DK, G_DV), steps)
    y_s = _peer(_merge(xs, o_as, o_bs, gos, mas, mbs, pp), pp)

    kv_p = (1, batch, seq, A_HEADS, A_HEAD_DIM)
    kv_s = (1, nseq, steps, A_HEADS, A_HEAD_DIM)
    return (y_p.reshape(batch, seq, d), y_s.reshape(nseq, steps, d),
            k.reshape(kv_p), v.reshape(kv_p), st_p[None],
            ks.reshape(kv_s), vs.reshape(kv_s), st_s[None])
```

```python
import functools
import math

import jax
import jax.numpy as jnp
from jax import lax
from jax.experimental import pallas as pl
from jax.experimental.pallas import tpu as pltpu

F32 = jnp.float32
BF16 = jnp.bfloat16

D_MODEL = 1024
PAST_LEN = 8192
PAGE_SIZE = 128
A_HEADS = 8
A_HEAD_DIM = 64
A_WIDTH = A_HEADS * A_HEAD_DIM
MOBA_BLOCK = 256
MOBA_TOPK = 3
ROPE_THETA = 10000.0
G_HEADS = 4
G_DK = 64
G_DV = 128
G_KEY_WIDTH = G_HEADS * G_DK
G_VAL_WIDTH = G_HEADS * G_DV
G_GATE_RANK = 16
G_GATE_NORMALIZER = 16.0
G_CHUNK = 64
G_SUB = 16
P_HEADS = 8
P_N_KEYS = 128
P_HALF = 128
P_TOPK = 16
EPS = 1e-6

LANES = 128
VMEM_LIMIT = 56 * 1024 * 1024

TOKEN_TILE = 256
GLR_PAD = LANES

NEG = -1e30


def _nt(a, b):
    return lax.dot_general(a, b, (((1,), (1,)), ((), ())), preferred_element_type=F32)


def _tn(a, b):
    return lax.dot_general(a, b, (((0,), (0,)), ((), ())), preferred_element_type=F32)


def _dot(a, b):
    return jnp.dot(a, b, preferred_element_type=F32)


def _split2(x):
    hi = x.astype(BF16)
    lo = (x - hi.astype(F32)).astype(BF16)
    return hi, lo


def _split3(x):
    hi = x.astype(BF16)
    r = x - hi.astype(F32)
    mid = r.astype(BF16)
    lo = (r - mid.astype(F32)).astype(BF16)
    return hi, mid, lo


_SEG_Q, _SEG_K, _SEG_V = 0, 512, 1024
_SEG_GQ, _SEG_GK, _SEG_GV = 1536, 1792, 2048
_SEG_GLR = 2560
_SEG_GOUT = _SEG_GLR + GLR_PAD
_SEG_MA = _SEG_GOUT + 512
_SEG_MB = _SEG_MA + 1024
_W_IN_PADDED = _SEG_MB + 1024


def _inproj_kernel(x_ref, g1_ref, w_ref, cos_ref, sin_ref, qn_ref, kn_ref, bd_ref, w2_ref, gb_ref,
                   q_ref, k_ref, v_ref, gq_ref, gk_ref, gv_ref, lg_ref, go_ref, ma_ref, mb_ref, km_ref):
    x = x_ref[...]
    xn = x * lax.rsqrt(jnp.mean(x * x, axis=-1, keepdims=True) + EPS) * g1_ref[...]
    xb = xn.astype(BF16)

    def proj(off, width):
        return _dot(xb, w_ref[:, off:off + width])

    cos = cos_ref[...]
    sin = sin_ref[...]
    lane = lax.broadcasted_iota(jnp.int32, (1, A_WIDTH), 1)
    first_half = (lane % A_HEAD_DIM) < (A_HEAD_DIM // 2)
    bd = bd_ref[...]

    def qk_epilogue(z, gamma):
        hi, lo = _split2(z * z)
        ms = (_dot(hi, bd) + _dot(lo, bd)) * (1.0 / A_HEAD_DIM)
        y = z * lax.rsqrt(ms + EPS) * gamma
        rot = jnp.where(first_half,
                        -pltpu.roll(y, A_WIDTH - A_HEAD_DIM // 2, 1),
                        pltpu.roll(y, A_HEAD_DIM // 2, 1))
        return y * cos + rot * sin

    q_ref[...] = qk_epilogue(proj(_SEG_Q, A_WIDTH), qn_ref[...])
    k = qk_epilogue(proj(_SEG_K, A_WIDTH), kn_ref[...])
    k_ref[...] = k
    km_ref[0] = jnp.mean(k, axis=0, keepdims=True)
    v_ref[...] = proj(_SEG_V, A_WIDTH)
    gq_ref[...] = proj(_SEG_GQ, G_KEY_WIDTH) * (G_DK ** -0.5)
    gk_ref[...] = proj(_SEG_GK, G_KEY_WIDTH)
    gv_ref[...] = proj(_SEG_GV, G_VAL_WIDTH)
    glr = proj(_SEG_GLR, GLR_PAD)
    t = _dot(glr.astype(BF16), w2_ref[...]) + gb_ref[...]
    lg_ref[...] = (jnp.minimum(t, 0.0) - jnp.log1p(jnp.exp(-jnp.abs(t)))) * (1.0 / G_GATE_NORMALIZER)
    go_ref[...] = proj(_SEG_GOUT, G_VAL_WIDTH)
    ma_ref[...] = proj(_SEG_MA, D_MODEL)
    mb_ref[...] = proj(_SEG_MB, D_MODEL)


def _inproj(x, pos_block_of_tile, cos_tab, sin_tab, p):
    n = x.shape[0]
    tm = TOKEN_TILE
    nt = n // tm
    row = lambda i: (i, 0)
    const = lambda i: (0, 0)
    widths = (A_WIDTH, A_WIDTH, A_WIDTH, G_KEY_WIDTH, G_KEY_WIDTH, G_VAL_WIDTH, G_KEY_WIDTH,
              G_VAL_WIDTH, D_MODEL, D_MODEL)
    out_shape = [jax.ShapeDtypeStruct((n, w), F32) for w in widths]
    out_specs = [pl.BlockSpec((tm, w), row) for w in widths]
    out_shape.append(jax.ShapeDtypeStruct((nt, 1, A_WIDTH), F32))
    out_specs.append(pl.BlockSpec((1, 1, A_WIDTH), lambda i: (i, 0, 0)))
    tab_spec = pl.BlockSpec((tm, A_WIDTH), lambda i: (pos_block_of_tile(i), 0))
    return pl.pallas_call(
        _inproj_kernel,
        grid=(nt,),
        in_specs=[
            pl.BlockSpec((tm, D_MODEL), row),
            pl.BlockSpec((1, D_MODEL), const),
            pl.BlockSpec((D_MODEL, _W_IN_PADDED), const),
            tab_spec, tab_spec,
            pl.BlockSpec((1, A_WIDTH), const),
            pl.BlockSpec((1, A_WIDTH), const),
            pl.BlockSpec((A_WIDTH, A_WIDTH), const),
            pl.BlockSpec((GLR_PAD, G_KEY_WIDTH), const),
            pl.BlockSpec((1, G_KEY_WIDTH), const),
        ],
        out_specs=out_specs,
        out_shape=out_shape,
        compiler_params=pltpu.CompilerParams(dimension_semantics=("arbitrary",), vmem_limit_bytes=VMEM_LIMIT),
        name="inproj",
    )(x, p["g1"], p["w_in"], cos_tab, sin_tab, p["qn"], p["kn"], p["bd"], p["w2"], p["gb"])


def _rope_tables(pos):
    half = A_HEAD_DIM // 2
    inv_freq = ROPE_THETA ** (-jnp.arange(half, dtype=F32) / half)
    ang = pos.astype(F32)[:, None] * inv_freq[None, :]
    reps = A_WIDTH // half
    return jnp.tile(jnp.cos(ang), (1, reps)), jnp.tile(jnp.sin(ang), (1, reps))


def _prep_mixer_params(norm1_g, w_in, a_q_norm, a_k_norm, g_gate_w2, g_gate_b):
    glr_off = 3 * A_WIDTH + 2 * G_KEY_WIDTH + G_VAL_WIDTH
    w_pad = jnp.concatenate([
        w_in[:, :glr_off],
        jnp.pad(w_in[:, glr_off:glr_off + G_GATE_RANK], ((0, 0), (0, GLR_PAD - G_GATE_RANK))),
        w_in[:, glr_off + G_GATE_RANK:]], axis=1).astype(BF16)
    head_id = jnp.arange(A_WIDTH) // A_HEAD_DIM
    return {
        "g1": norm1_g.reshape(1, D_MODEL),
        "w_in": w_pad,
        "qn": jnp.tile(a_q_norm, A_HEADS).reshape(1, A_WIDTH),
        "kn": jnp.tile(a_k_norm, A_HEADS).reshape(1, A_WIDTH),
        "bd": (head_id[:, None] == head_id[None, :]).astype(BF16),
        "w2": jnp.pad(g_gate_w2, ((0, GLR_PAD - G_GATE_RANK), (0, 0))).astype(BF16),
        "gb": g_gate_b.reshape(1, G_KEY_WIDTH),
    }


def _topk_rows_mask(s, valid, n_sel):
    n = s.shape[0]
    sm = jnp.where(valid, s, -jnp.inf)
    ridx = lax.broadcasted_iota(jnp.int32, s.shape, 0)
    cnt = jnp.zeros(s.shape, jnp.int32)
    for r in range(n):
        row = sm[r:r + 1, :]
        beats = (row > sm) | ((row == sm) & (r < ridx))
        cnt = cnt + beats.astype(jnp.int32)
    return valid & (cnt < n_sel)


PROMPT_BLOCKS_PER_ITER = 4


def _moba_prompt_kernel(q_ref, k_ref, v_ref, km_ref, o_ref, sel_ref):
    ob = pl.program_id(2)
    blk = MOBA_BLOCK
    nb = km_ref.shape[0]
    scale = A_HEAD_DIM ** -0.5
    kpos = lax.broadcasted_iota(jnp.int32, (blk, blk), 0)
    qpos = lax.broadcasted_iota(jnp.int32, (blk, blk), 1)
    causal = kpos <= qpos
    n_heads = LANES // A_HEAD_DIM
    head_lanes = [slice(hh * A_HEAD_DIM, (hh + 1) * A_HEAD_DIM) for hh in range(n_heads)]
    nidx = lax.broadcasted_iota(jnp.int32, (nb, blk), 0)
    qs = []
    for hh, sl in enumerate(head_lanes):
        q = q_ref[:, sl]
        s_sel = lax.dot_general(km_ref[:, sl], q, (((1,), (1,)), ((), ())),
                                precision=lax.Precision.HIGHEST, preferred_element_type=F32)
        sel_ref[hh] = _topk_rows_mask(s_sel, nidx < ob, MOBA_TOPK).astype(F32)
        qs.append((q * scale).astype(BF16))

    def scores(n, hh):
        kb = k_ref[pl.ds(n * blk, blk), head_lanes[hh]].astype(BF16)
        return _nt(kb, qs[hh])

    def values_t(n, hh):
        return v_ref[pl.ds(n * blk, blk), head_lanes[hh]].T.astype(BF16)

    init = []
    for hh in range(n_heads):
        s0 = jnp.where(causal, scores(ob, hh), -jnp.inf)
        m0 = jnp.max(s0, axis=0, keepdims=True)
        p0 = jnp.exp(s0 - m0)
        init.append((m0, jnp.sum(p0, axis=0, keepdims=True), _dot(values_t(ob, hh), p0.astype(BF16))))

    def body(it, carry):
        new = []
        for hh in range(n_heads):
            m, l, acc = carry[hh]
            blocks = [jnp.minimum(it * PROMPT_BLOCKS_PER_ITER + j, nb - 1)
                      for j in range(PROMPT_BLOCKS_PER_ITER)]
            ss = [jnp.where(sel_ref[hh, pl.ds(n, 1), :] > 0.5, scores(n, hh), -jnp.inf) for n in blocks]
            m_new = m
            for s in ss:
                m_new = jnp.maximum(m_new, jnp.max(s, axis=0, keepdims=True))
            alpha = jnp.exp(m - m_new)
            l = alpha * l
            acc = alpha * acc
            for n, s in zip(blocks, ss):
                p = jnp.exp(s - m_new)
                l = l + jnp.sum(p, axis=0, keepdims=True)
                acc = acc + _dot(values_t(n, hh), p.astype(BF16))
            new.append((m_new, l, acc))
        return tuple(new)

    n_iter = (ob + PROMPT_BLOCKS_PER_ITER - 1) // PROMPT_BLOCKS_PER_ITER
    final = lax.fori_loop(0, n_iter, body, tuple(init))
    o_ref[...] = jnp.concatenate([(acc / l).T for _, l, acc in final], axis=1)


def _moba_prompt(q, k, v, kmeans, batch, seq):
    nb = seq // MOBA_BLOCK
    hp = A_WIDTH // LANES
    return pl.pallas_call(
        _moba_prompt_kernel,
        grid=(batch, hp, nb),
        in_specs=[
            pl.BlockSpec((MOBA_BLOCK, LANES), lambda b, h, i: (b * nb + i, h)),
            pl.BlockSpec((seq, LANES), lambda b, h, i: (b, h)),
            pl.BlockSpec((seq, LANES), lambda b, h, i: (b, h)),
            pl.BlockSpec((nb, LANES), lambda b, h, i: (b, h)),
        ],
        out_specs=pl.BlockSpec((MOBA_BLOCK, LANES), lambda b, h, i: (b * nb + i, h)),
        out_shape=jax.ShapeDtypeStruct((batch * seq, A_WIDTH), F32),
        scratch_shapes=[pltpu.VMEM((LANES // A_HEAD_DIM, nb, MOBA_BLOCK), F32)],
        compiler_params=pltpu.CompilerParams(
            dimension_semantics=("arbitrary", "arbitrary", "arbitrary"), vmem_limit_bytes=VMEM_LIMIT),
        name="moba_prompt",
    )(q, k, v, kmeans)


def _cumsum_rows(tri, g):
    hi, mid, lo = _split3(g)
    return _dot(tri, hi) + _dot(tri, mid) + _dot(tri, lo)


GLA_TILE = 128


def _gla_prompt_kernel(q_ref, k_ref, v_ref, g_ref, tri_ref, o_ref, s_ref, st_ref):
    t = pl.program_id(0)
    c, sub = G_CHUNK, G_SUB
    batch = q_ref.shape[0]

    @pl.when(t == 0)
    def _():
        st_ref[...] = jnp.zeros_like(st_ref)

    tri = tri_ref[...]
    key_head = lax.broadcasted_iota(jnp.int32, (1, G_KEY_WIDTH), 1) // G_DK
    blockdiag = (lax.broadcasted_iota(jnp.int32, (G_VAL_WIDTH, G_KEY_WIDTH), 0) // G_DV
                 == lax.broadcasted_iota(jnp.int32, (G_VAL_WIDTH, G_KEY_WIDTH), 1) // G_DK)
    for ci, bi in [(ci, bi) for ci in range(GLA_TILE // c) for bi in range(batch)]:
        rows = slice(ci * c, (ci + 1) * c)
        q, k, v, g = q_ref[bi, rows, :], k_ref[bi, rows, :], v_ref[bi, rows, :], g_ref[bi, rows, :]
        b = _cumsum_rows(tri, g)
        blast = b[c - 1:c, :]
        q_inter = (q * jnp.exp(b)).astype(BF16)
        k_dec = (k * jnp.exp(blast - b)).astype(BF16)
        vb = v.astype(BF16)
        st = st_ref[bi]
        o_rows = []
        for i in range(c // sub):
            bref = b[i * sub - 1:i * sub, :] if i > 0 else jnp.zeros((1, G_KEY_WIDTH), F32)
            r = slice(i * sub, (i + 1) * sub)
            ncol = (i + 1) * sub
            q_sub = q[r, :] * jnp.exp(b[r, :] - bref)
            k_sub = (k[:ncol, :] * jnp.exp(bref - b[:ncol, :])).astype(BF16)
            q_stack = jnp.concatenate(
                [jnp.where(key_head == h, q_sub, 0.0) for h in range(G_HEADS)], axis=0).astype(BF16)
            a = _nt(q_stack, k_sub)
            rr = lax.broadcasted_iota(jnp.int32, (G_HEADS * sub, ncol), 0) % sub + i * sub
            cc = lax.broadcasted_iota(jnp.int32, (G_HEADS * sub, ncol), 1)
            av = _dot(jnp.where(cc <= rr, a, 0.0).astype(BF16), vb[:ncol, :])
            o_rows.append(jnp.concatenate(
                [av[h * sub:(h + 1) * sub, h * G_DV:(h + 1) * G_DV] for h in range(G_HEADS)], axis=1))
        o_ref[bi, rows, :] = _nt(q_inter, st.astype(BF16)) + jnp.concatenate(o_rows, axis=0)
        st_ref[bi] = st * jnp.exp(blast) + jnp.where(blockdiag, _tn(vb, k_dec), 0.0)

    @pl.when(t == pl.num_programs(0) - 1)
    def _():
        for bi in range(batch):
            for h in range(G_HEADS):
                s_ref[bi, h] = st_ref[bi, h * G_DV:(h + 1) * G_DV, h * G_DK:(h + 1) * G_DK].T


def _gla_prompt(gq, gk, gv, logg, batch, seq):
    tile = lambda w: pl.BlockSpec((batch, GLA_TILE, w), lambda t: (0, t, 0))
    tri = jnp.tril(jnp.ones((G_CHUNK, G_CHUNK), F32)).astype(BF16)
    shaped = lambda a: a.reshape(batch, seq, a.shape[-1])
    state_spec = pl.BlockSpec((batch, G_HEADS, G_DK, G_DV), lambda t: (0, 0, 0, 0))
    o, st = pl.pallas_call(
        _gla_prompt_kernel,
        grid=(seq // GLA_TILE,),
        in_specs=[tile(G_KEY_WIDTH), tile(G_KEY_WIDTH), tile(G_VAL_WIDTH), tile(G_KEY_WIDTH),
                  pl.BlockSpec((G_CHUNK, G_CHUNK), lambda t: (0, 0))],
        out_specs=[tile(G_VAL_WIDTH), state_spec],
        out_shape=[
            jax.ShapeDtypeStruct((batch, seq, G_VAL_WIDTH), F32),
            jax.ShapeDtypeStruct((batch, G_HEADS, G_DK, G_DV), F32),
        ],
        scratch_shapes=[pltpu.VMEM((batch, G_VAL_WIDTH, G_KEY_WIDTH), F32)],
        compiler_params=pltpu.CompilerParams(dimension_semantics=("arbitrary",), vmem_limit_bytes=VMEM_LIMIT),
        name="gla_prompt",
    )(shaped(gq), shaped(gk), shaped(gv), shaped(logg), tri)
    return o.reshape(batch * seq, G_VAL_WIDTH), st


GLA_SAMPLE_SEQS = 8


def _gla_sample_kernel(q_ref, k_ref, v_ref, g_ref, tri_ref, s0_ref, o_ref, s_ref, *, steps):
    q, k, v, g = q_ref[...], k_ref[...], v_ref[...], g_ref[...]
    b = _cumsum_rows(tri_ref[...], g)
    eb = jnp.exp(b)
    qd = (q * eb).astype(BF16)
    kinv = (k * jnp.exp(-b)).astype(BF16)
    vb = v.astype(BF16)
    rr = lax.broadcasted_iota(jnp.int32, (steps, steps), 0)
    cc = lax.broadcasted_iota(jnp.int32, (steps, steps), 1)
    eye = lax.broadcasted_iota(jnp.int32, (G_DK, G_DK), 0) == lax.broadcasted_iota(jnp.int32, (G_DK, G_DK), 1)
    out_rows = []
    for s in range(GLA_SAMPLE_SEQS):
        rows = slice(s * steps, (s + 1) * steps)
        blast = b[(s + 1) * steps - 1:(s + 1) * steps, :]
        k_dec = (k[rows, :] * jnp.exp(blast - b[rows, :])).astype(BF16)
        outs = []
        for h in range(G_HEADS):
            hs = slice(h * G_DK, (h + 1) * G_DK)
            vs = slice(h * G_DV, (h + 1) * G_DV)
            s0 = s0_ref[s, h]
            a = jnp.where(cc <= rr, _nt(qd[rows, hs], kinv[rows, hs]), 0.0).astype(BF16)
            outs.append(_dot(qd[rows, hs], s0.astype(BF16)) + _dot(a, vb[rows, vs]))
            dcol = jnp.sum(jnp.where(eye, jnp.exp(blast[:, hs]), 0.0), axis=1, keepdims=True)
            s_ref[s, h] = dcol * s0 + _tn(k_dec[:, hs], vb[rows, vs])
        out_rows.append(jnp.concatenate(outs, axis=1))
    o_ref[...] = jnp.concatenate(out_rows, axis=0)


def _gla_sample(gq, gk, gv, logg, state, steps):
    nseq = state.shape[0]
    sb = GLA_SAMPLE_SEQS
    rows = sb * steps
    idx = jnp.arange(rows)
    tri = ((idx[:, None] >= idx[None, :]) & (idx[:, None] // steps == idx[None, :] // steps)).astype(BF16)
    row = lambda i: (i, 0)
    return pl.pallas_call(
        functools.partial(_gla_sample_kernel, steps=steps),
        grid=(nseq // sb,),
        in_specs=[
            pl.BlockSpec((rows, G_KEY_WIDTH), row),
            pl.BlockSpec((rows, G_KEY_WIDTH), row),
            pl.BlockSpec((rows, G_VAL_WIDTH), row),
            pl.BlockSpec((rows, G_KEY_WIDTH), row),
            pl.BlockSpec((rows, rows), lambda i: (0, 0)),
            pl.BlockSpec((sb, G_HEADS, G_DK, G_DV), lambda i: (i, 0, 0, 0)),
        ],
        out_specs=[
            pl.BlockSpec((rows, G_VAL_WIDTH), row),
            pl.BlockSpec((sb, G_HEADS, G_DK, G_DV), lambda i: (i, 0, 0, 0)),
        ],
        out_shape=[
            jax.ShapeDtypeStruct((nseq * steps, G_VAL_WIDTH), F32),
            jax.ShapeDtypeStruct(state.shape, F32),
        ],
        compiler_params=pltpu.CompilerParams(dimension_semantics=("arbitrary",), vmem_limit_bytes=VMEM_LIMIT),
        name="gla_sample",
    )(gq, gk, gv, logg, tri, state)


PAGES_PER_BLOCK = MOBA_BLOCK // PAGE_SIZE


SAMPLE_PAGES_PER_STEP = 16


def _moba_sample_kernel(pt_ref, q_ref, kn_ref, vn_ref, *refs, steps, n_blocks):
    del pt_ref
    pps = SAMPLE_PAGES_PER_STEP
    k_refs, v_refs = refs[:pps], refs[pps:2 * pps]
    o_ref, wq_ref, m_ref, l_ref, ss_ref, acc_ref = refs[2 * pps:]
    g = pl.program_id(1)
    nq = A_HEADS * steps
    scale = A_HEAD_DIM ** -0.5

    @pl.when(g == 0)
    def _():
        qt = jnp.concatenate([q_ref[...]] * A_HEADS, axis=0)
        rh = lax.broadcasted_iota(jnp.int32, (nq, A_WIDTH), 0) // steps
        ch = lax.broadcasted_iota(jnp.int32, (nq, A_WIDTH), 1) // A_HEAD_DIM
        wq_ref[...] = jnp.where(rh == ch, qt * scale, 0.0).astype(BF16)

    wq = wq_ref[...]
    for bi in range(pps // PAGES_PER_BLOCK):
        n = g * (pps // PAGES_PER_BLOCK) + bi
        pages = range(bi * PAGES_PER_BLOCK, (bi + 1) * PAGES_PER_BLOCK)
        st = jnp.concatenate([_dot(wq, k_refs[j][0].astype(BF16)).T for j in pages], axis=0)
        m = jnp.max(st, axis=0, keepdims=True)
        p = jnp.exp(st - m)
        m_ref[pl.ds(n, 1), :] = m
        l_ref[pl.ds(n, 1), :] = jnp.sum(p, axis=0, keepdims=True)
        ss_ref[pl.ds(n, 1), :] = jnp.sum(st, axis=0, keepdims=True)
        pb = p.astype(BF16)
        acc = None
        for jj, j in enumerate(pages):
            part = _dot(v_refs[j][0].astype(BF16), pb[jj * PAGE_SIZE:(jj + 1) * PAGE_SIZE, :])
            acc = part if acc is None else acc + part
        acc_ref[n] = acc

    @pl.when(g == pl.num_programs(1) - 1)
    def _():
        ss = ss_ref[...]
        sel = _topk_rows_mask(ss, jnp.ones(ss.shape, jnp.bool_), min(MOBA_TOPK, n_blocks))
        s_own = _nt(kn_ref[...].astype(BF16), wq)
        tk = lax.broadcasted_iota(jnp.int32, (steps, nq), 0)
        tq = lax.broadcasted_iota(jnp.int32, (steps, nq), 1) % steps
        s_own = jnp.where(tk <= tq, s_own, -jnp.inf)
        mm = jnp.where(sel, m_ref[...], -jnp.inf)
        big = jnp.maximum(jnp.max(mm, axis=0, keepdims=True), jnp.max(s_own, axis=0, keepdims=True))
        w = jnp.where(sel, jnp.exp(mm - big), 0.0)
        p_own = jnp.exp(s_own - big)
        denom = jnp.sum(w * l_ref[...], axis=0, keepdims=True) + jnp.sum(p_own, axis=0, keepdims=True)
        acc = _tn(vn_ref[...].astype(BF16), p_own.astype(BF16))
        for j in range(n_blocks):
            acc = acc + w[j:j + 1, :] * acc_ref[j]
        out = (acc / denom).T
        o_ref[...] = jnp.concatenate(
            [out[h * steps:(h + 1) * steps, h * A_HEAD_DIM:(h + 1) * A_HEAD_DIM] for h in range(A_HEADS)], axis=1)


def _moba_sample(q, k_new, v_new, cache_kt, cache_vt, page_table, steps):
    nseq, n_pages = page_table.shape
    n_blocks = n_pages // PAGES_PER_BLOCK
    pps = SAMPLE_PAGES_PER_STEP
    nq = A_HEADS * steps
    tok = pl.BlockSpec((steps, A_WIDTH), lambda s, g, pt: (s, 0))

    def page(j):
        return pl.BlockSpec((1, A_WIDTH, PAGE_SIZE), lambda s, g, pt: (pt[s * n_pages + g * pps + j], 0, 0))

    pages = [page(j) for j in range(pps)]
    grid_spec = pltpu.PrefetchScalarGridSpec(
        num_scalar_prefetch=1,
        grid=(nseq, n_pages // pps),
        in_specs=[tok, tok, tok] + pages + pages,
        out_specs=tok,
        scratch_shapes=[
            pltpu.VMEM((nq, A_WIDTH), BF16),
            pltpu.VMEM((n_blocks, nq), F32),
            pltpu.VMEM((n_blocks, nq), F32),
            pltpu.VMEM((n_blocks, nq), F32),
            pltpu.VMEM((n_blocks, A_WIDTH, nq), F32),
        ],
    )
    return pl.pallas_call(
        functools.partial(_moba_sample_kernel, steps=steps, n_blocks=n_blocks),
        grid_spec=grid_spec,
        out_shape=jax.ShapeDtypeStruct((nseq * steps, A_WIDTH), F32),
        compiler_params=pltpu.CompilerParams(
            dimension_semantics=("arbitrary", "arbitrary"), vmem_limit_bytes=VMEM_LIMIT),
        name="moba_sample",
    )(page_table.reshape(-1), q, k_new, v_new, *([cache_kt] * pps), *([cache_vt] * pps))


def _merge_kernel(x_ref, oa_ref, ob_ref, go_ref, ma_ref, mb_ref, gn_ref, wa_ref, wb_ref, wo_ref, h_ref):
    ya = _dot(oa_ref[...].astype(BF16), wa_ref[...])
    ob = ob_ref[...]
    normed = []
    for h in range(G_HEADS):
        seg = ob[:, h * G_DV:(h + 1) * G_DV]
        normed.append(seg * lax.rsqrt(jnp.mean(seg * seg, axis=-1, keepdims=True) + EPS))
    go = go_ref[...]
    ob = jnp.concatenate(normed, axis=1) * gn_ref[...] * (go * jax.nn.sigmoid(go))
    yb = _dot(ob.astype(BF16), wb_ref[...])
    merged = jax.nn.sigmoid(ma_ref[...]) * ya + jax.nn.sigmoid(mb_ref[...]) * yb
    h_ref[...] = x_ref[...] + _dot(merged.astype(BF16), wo_ref[...])


def _merge(x, o_a, o_b, gout, ma, mb, p):
    n = x.shape[0]
    tm = TOKEN_TILE
    row = lambda i: (i, 0)
    const = lambda i: (0, 0)
    return pl.pallas_call(
        _merge_kernel,
        grid=(n // tm,),
        in_specs=[
            pl.BlockSpec((tm, D_MODEL), row),
            pl.BlockSpec((tm, A_WIDTH), row),
            pl.BlockSpec((tm, G_VAL_WIDTH), row),
            pl.BlockSpec((tm, G_VAL_WIDTH), row),
            pl.BlockSpec((tm, D_MODEL), row),
            pl.BlockSpec((tm, D_MODEL), row),
            pl.BlockSpec((1, G_VAL_WIDTH), const),
            pl.BlockSpec((A_WIDTH, D_MODEL), const),
            pl.BlockSpec((G_VAL_WIDTH, D_MODEL), const),
            pl.BlockSpec((D_MODEL, D_MODEL), const),
        ],
        out_specs=pl.BlockSpec((tm, D_MODEL), row),
        out_shape=jax.ShapeDtypeStruct((n, D_MODEL), F32),
        compiler_params=pltpu.CompilerParams(dimension_semantics=("arbitrary",), vmem_limit_bytes=VMEM_LIMIT),
        name="merge",
    )(x, o_a, o_b, gout, ma, mb, p["gn"], p["wa"], p["wb"], p["wo"])


PEER_TOKEN_TILE = 512
PEER_KEYS_PER_STEP = 8
PEER_EXPERT_TILE = PEER_KEYS_PER_STEP * P_N_KEYS
PEER_GATE_GROUP = 4
PEER_GATE_KEY_SPLIT = 2
BIG = 1e30


def _top_rows(s, count):
    cur = s
    vals = []
    for _ in range(count):
        m = jnp.max(cur, axis=0, keepdims=True)
        vals.append(m)
        cur = jnp.where(cur >= m, -jnp.inf, cur)
    return vals


def _peer_kernel(h_ref, g2_ref, wq_ref, sk1_ref, sk2_ref, u_ref, vt_ref, o_ref,
                 hb_ref, th_ref, a_ref, s2_ref, c_ref, acc_ref, act_ref, p_ref, ths_ref, as_ref):
    e = pl.program_id(1)
    tm = h_ref.shape[0]

    @pl.when(e == 0)
    def _():
        hx = h_ref[...]
        hn = hx * lax.rsqrt(jnp.mean(hx * hx, axis=-1, keepdims=True) + EPS) * g2_ref[...]
        hb = hn.astype(BF16)
        hb_ref[...] = hb
        for hd in range(P_HEADS):
            qt = _nt(wq_ref[hd * 2 * P_HALF:(hd + 1) * 2 * P_HALF, :], hb)
            s1 = _dot(sk1_ref[hd], qt[:P_HALF].astype(BF16))
            s2 = _dot(sk2_ref[hd], qt[P_HALF:].astype(BF16))
            v1 = _top_rows(s1, P_TOPK)
            v2 = _top_rows(s2, P_TOPK)
            cand = [v1[a] + v2[b] for a in range(P_TOPK) for b in range(P_TOPK // (a + 1))]
            pad = -len(cand) % 8
            cand = jnp.concatenate(cand + [jnp.full((pad, tm), -jnp.inf, F32)], axis=0)
            tops = _top_rows(cand, P_TOPK + 1)
            cut = 0.5 * (tops[P_TOPK - 1] + tops[P_TOPK])
            z = jnp.sum(jnp.where(cand >= tops[P_TOPK - 1], jnp.exp(cand - tops[0]), 0.0),
                        axis=0, keepdims=True)
            th = jnp.where(s1 >= v1[P_TOPK - 1], cut - s1, BIG)
            aa = jnp.exp(s1 - v1[0]) / z
            s2m = jnp.where(s2 >= v2[P_TOPK - 1], s2, -BIG)
            cc2 = jnp.exp(s2 - v2[0])
            for cc in range(tm // LANES):
                cols = slice(cc * LANES, (cc + 1) * LANES)
                th_ref[hd, cc] = th[:, cols]
                a_ref[hd, cc] = aa[:, cols]
                s2_ref[hd, cc] = s2m[:, cols]
                c_ref[hd, cc] = cc2[:, cols]
        acc_ref[...] = jnp.zeros_like(acc_ref)

    n_cc = tm // LANES
    group = PEER_GATE_GROUP
    half = P_N_KEYS // PEER_GATE_KEY_SPLIT

    base = pl.multiple_of(e * PEER_KEYS_PER_STEP, PEER_KEYS_PER_STEP)
    ths_ref[...] = th_ref[:, :, pl.ds(base, PEER_KEYS_PER_STEP), :]
    as_ref[...] = a_ref[:, :, pl.ds(base, PEER_KEYS_PER_STEP), :]

    def gate_unit(grp, cc, act_r, p_w):
        cols = slice(cc * LANES, (cc + 1) * LANES)
        for jh in range(PEER_GATE_KEY_SPLIT):
            jr = slice(jh * half, (jh + 1) * half)
            g = [jnp.zeros((half, LANES), F32) for _ in range(group)]
            for hd in range(P_HEADS):
                s2h = s2_ref[hd, cc, jr, :]
                ch = c_ref[hd, cc, jr, :]
                for k in range(group):
                    i = grp * group + k
                    th = ths_ref[hd, cc, i:i + 1, :]
                    aa = as_ref[hd, cc, i:i + 1, :]
                    g[k] = g[k] + jnp.where(s2h >= th, ch, 0.0) * aa
            for k in range(group):
                r0 = (grp * group + k) * P_N_KEYS + jh * half
                act = act_r[r0:r0 + half, cols]
                gl = 0.5 * act * (1.0 + lax.erf(act * (1.0 / math.sqrt(2.0))))
                p_w[r0:r0 + half, cols] = (g[k] * gl).astype(BF16)

    act_ref[...] = _nt(u_ref[...], hb_ref[...])
    for grp in range(PEER_KEYS_PER_STEP // group):
        for cc in range(n_cc):
            gate_unit(grp, cc, act_ref, p_ref)
    acc_ref[...] += _dot(vt_ref[...], p_ref[...])

    @pl.when(e == pl.num_programs(1) - 1)
    def _():
        o_ref[...] = h_ref[...] + acc_ref[...].T


def _peer(h, p):
    n = h.shape[0]
    tm = PEER_TOKEN_TILE
    te = PEER_EXPERT_TILE
    n_et = p["u"].shape[0] // te
    row = lambda t, e: (t, 0)
    const2 = lambda t, e: (0, 0)
    const3 = lambda t, e: (0, 0, 0)
    sel_scratch = pltpu.VMEM((P_HEADS, tm // LANES, P_N_KEYS, LANES), F32)
    return pl.pallas_call(
        _peer_kernel,
        grid=(n // tm, n_et),
        in_specs=[
            pl.BlockSpec((tm, D_MODEL), row),
            pl.BlockSpec((1, D_MODEL), const2),
            pl.BlockSpec((P_HEADS * 2 * P_HALF, D_MODEL), const2),
            pl.BlockSpec((P_HEADS, P_N_KEYS, P_HALF), const3),
            pl.BlockSpec((P_HEADS, P_N_KEYS, P_HALF), const3),
            pl.BlockSpec((te, D_MODEL), lambda t, e: (e, 0)),
            pl.BlockSpec((D_MODEL, te), lambda t, e: (0, e)),
        ],
        out_specs=pl.BlockSpec((tm, D_MODEL), row),
        out_shape=jax.ShapeDtypeStruct((n, D_MODEL), F32),
        scratch_shapes=[
            pltpu.VMEM((tm, D_MODEL), BF16),
            sel_scratch, sel_scratch, sel_scratch, sel_scratch,
            pltpu.VMEM((D_MODEL, tm), F32),
            pltpu.VMEM((te, tm), F32),
            pltpu.VMEM((te, tm), BF16),
            pltpu.VMEM((P_HEADS, tm // LANES, PEER_KEYS_PER_STEP, LANES), F32),
            pltpu.VMEM((P_HEADS, tm // LANES, PEER_KEYS_PER_STEP, LANES), F32),
        ],
        compiler_params=pltpu.CompilerParams(
            dimension_semantics=("arbitrary", "arbitrary"), vmem_limit_bytes=VMEM_LIMIT),
        name="peer",
    )(h, p["g2"], p["wq_t"], p["sk1"], p["sk2"], p["u"], p["v_t"])


def _prep_post_params(g_out_norm, w_branch_a, w_branch_b, w_out, norm2_g, peer_wq, sk1, sk2, peer_u, peer_v):
    return {
        "gn": jnp.tile(g_out_norm, G_HEADS).reshape(1, G_VAL_WIDTH),
        "wa": w_branch_a.astype(BF16),
        "wb": w_branch_b.astype(BF16),
        "wo": w_out.astype(BF16),
        "g2": norm2_g.reshape(1, D_MODEL),
        "wq_t": peer_wq.T.astype(BF16),
        "sk1": sk1.astype(BF16),
        "sk2": sk2.astype(BF16),
        "u": peer_u.astype(BF16),
        "v_t": peer_v.T.astype(BF16),
    }


def kernel(x_prompt, x_sample, cache_k, cache_v, state_gla, page_table, norm1_g, w_in, a_q_norm, a_k_norm,
           g_gate_w2, g_gate_b, g_out_norm, w_branch_a, w_branch_b, w_out, norm2_g, peer_wq,
           peer_subkeys1, peer_subkeys2, peer_u, peer_v):
    batch, seq, d = x_prompt.shape
    nseq, steps, _ = x_sample.shape
    depth, n_phys = cache_k.shape[:2]
    assert depth == 1 and d == D_MODEL
    assert seq % MOBA_BLOCK == 0 and MOBA_BLOCK == TOKEN_TILE
    past_len = page_table.shape[1] * PAGE_SIZE
    assert past_len % MOBA_BLOCK == 0 and TOKEN_TILE % steps == 0 and (nseq * steps) % TOKEN_TILE == 0
    assert math.gcd(steps, G_CHUNK) == steps and nseq % GLA_SAMPLE_SEQS == 0

    mp = _prep_mixer_params(norm1_g[0], w_in[0], a_q_norm[0], a_k_norm[0], g_gate_w2[0], g_gate_b[0])
    pp = _prep_post_params(g_out_norm[0], w_branch_a[0], w_branch_b[0], w_out[0], norm2_g[0], peer_wq[0],
                           peer_subkeys1[0], peer_subkeys2[0], peer_u[0], peer_v[0])

    nb = seq // TOKEN_TILE
    cos_p, sin_p = _rope_tables(jnp.arange(seq, dtype=jnp.int32))
    xp = x_prompt.reshape(batch * seq, d)
    q, k, v, gq, gk, gv, lg, go, ma, mb, km = _inproj(xp, lambda i: i % nb, cos_p, sin_p, mp)
    o_a = _moba_prompt(q, k, v, km.reshape(batch * nb, A_WIDTH), batch, seq)
    o_b, st_p = _gla_prompt(gq, gk, gv, lg, batch, seq)
    y_p = _peer(_merge(xp, o_a, o_b, go, ma, mb, pp), pp)

    pos_s = past_len + jnp.arange(TOKEN_TILE, dtype=jnp.int32) % steps
    cos_s, sin_s = _rope_tables(pos_s)
    xs = x_sample.reshape(nseq * steps, d)
    qs, ks, vs, gqs, gks, gvs, lgs, gos, mas, mbs, _ = _inproj(xs, lambda i: 0, cos_s, sin_s, mp)
    def pages_t(c):
        c = c.reshape(n_phys, PAGE_SIZE, A_HEADS, A_HEAD_DIM)
        return jnp.transpose(c, (0, 2, 3, 1)).reshape(n_phys, A_WIDTH, PAGE_SIZE)

    o_as = _moba_sample(qs, ks, vs, pages_t(cache_k), pages_t(cache_v), page_table, steps)
    o_bs, st_s = _gla_sample(gqs, gks, gvs, lgs, state_gla.reshape(nseq, G_HEADS, G_DK, G_DV), steps)
    y_s = _peer(_merge(xs, o_as, o_bs, gos, mas, mbs, pp), pp)

    kv_p = (1, batch, seq, A_HEADS, A_HEAD_DIM)
    kv_s = (1, nseq, steps, A_HEADS, A_HEAD_DIM)
    return (y_p.reshape(batch, seq, d), y_s.reshape(nseq, steps, d),
            k.reshape(kv_p), v.reshape(kv_p), st_p[None],
            ks.reshape(kv_s), vs.reshape(kv_s), st_s[None])
```

```python
import functools
import math

import jax
import jax.numpy as jnp
from jax import lax
from jax.experimental import pallas as pl
from jax.experimental.pallas import tpu as pltpu

F32 = jnp.float32
BF16 = jnp.bfloat16

D_MODEL = 1024
PAST_LEN = 8192
PAGE_SIZE = 128
A_HEADS = 8
A_HEAD_DIM = 64
A_WIDTH = A_HEADS * A_HEAD_DIM
MOBA_BLOCK = 256
MOBA_TOPK = 3
ROPE_THETA = 10000.0
G_HEADS = 4
G_DK = 64
G_DV = 128
G_KEY_WIDTH = G_HEADS * G_DK
G_VAL_WIDTH = G_HEADS * G_DV
G_GATE_RANK = 16
G_GATE_NORMALIZER = 16.0
G_CHUNK = 64
G_SUB = 16
P_HEADS = 8
P_N_KEYS = 128
P_HALF = 128
P_TOPK = 16
EPS = 1e-6

LANES = 128
VMEM_LIMIT = 56 * 1024 * 1024

TOKEN_TILE = 256
GLR_PAD = LANES

NEG = -1e30


def _nt(a, b):
    return lax.dot_general(a, b, (((1,), (1,)), ((), ())), preferred_element_type=F32)


def _tn(a, b):
    return lax.dot_general(a, b, (((0,), (0,)), ((), ())), preferred_element_type=F32)


def _dot(a, b):
    return jnp.dot(a, b, preferred_element_type=F32)


def _split2(x):
    hi = x.astype(BF16)
    lo = (x - hi.astype(F32)).astype(BF16)
    return hi, lo


def _split3(x):
    hi = x.astype(BF16)
    r = x - hi.astype(F32)
    mid = r.astype(BF16)
    lo = (r - mid.astype(F32)).astype(BF16)
    return hi, mid, lo


_SEG_Q, _SEG_K, _SEG_V = 0, 512, 1024
_SEG_GQ, _SEG_GK, _SEG_GV = 1536, 1792, 2048
_SEG_GLR = 2560
_SEG_GOUT = _SEG_GLR + GLR_PAD
_SEG_MA = _SEG_GOUT + 512
_SEG_MB = _SEG_MA + 1024
_W_IN_PADDED = _SEG_MB + 1024


def _inproj_kernel(x_ref, g1_ref, w_ref, cos_ref, sin_ref, qn_ref, kn_ref, bd_ref, w2_ref, gb_ref,
                   q_ref, k_ref, v_ref, gq_ref, gk_ref, gv_ref, lg_ref, go_ref, ma_ref, mb_ref, km_ref,
                   *kv_t_refs):
    x = x_ref[...]
    xn = x * lax.rsqrt(jnp.mean(x * x, axis=-1, keepdims=True) + EPS) * g1_ref[...]
    xb = xn.astype(BF16)

    def proj(off, width):
        return _dot(xb, w_ref[:, off:off + width])

    cos = cos_ref[...]
    sin = sin_ref[...]
    lane = lax.broadcasted_iota(jnp.int32, (1, A_WIDTH), 1)
    first_half = (lane % A_HEAD_DIM) < (A_HEAD_DIM // 2)
    bd = bd_ref[...]

    def qk_epilogue(z, gamma):
        hi, lo = _split2(z * z)
        ms = (_dot(hi, bd) + _dot(lo, bd)) * (1.0 / A_HEAD_DIM)
        y = z * lax.rsqrt(ms + EPS) * gamma
        rot = jnp.where(first_half,
                        -pltpu.roll(y, A_WIDTH - A_HEAD_DIM // 2, 1),
                        pltpu.roll(y, A_HEAD_DIM // 2, 1))
        return y * cos + rot * sin

    q_ref[...] = qk_epilogue(proj(_SEG_Q, A_WIDTH), qn_ref[...])
    k = qk_epilogue(proj(_SEG_K, A_WIDTH), kn_ref[...])
    k_ref[...] = k
    km_ref[0] = jnp.mean(k, axis=0, keepdims=True)
    v = proj(_SEG_V, A_WIDTH)
    v_ref[...] = v
    if kv_t_refs:
        kv_t_refs[0][0] = k.T
        kv_t_refs[1][0] = v.T
    gq_ref[...] = proj(_SEG_GQ, G_KEY_WIDTH) * (G_DK ** -0.5)
    gk_ref[...] = proj(_SEG_GK, G_KEY_WIDTH)
    gv_ref[...] = proj(_SEG_GV, G_VAL_WIDTH)
    glr = proj(_SEG_GLR, GLR_PAD)
    t = _dot(glr.astype(BF16), w2_ref[...]) + gb_ref[...]
    lg_ref[...] = (jnp.minimum(t, 0.0) - jnp.log1p(jnp.exp(-jnp.abs(t)))) * (1.0 / G_GATE_NORMALIZER)
    go_ref[...] = proj(_SEG_GOUT, G_VAL_WIDTH)
    ma_ref[...] = proj(_SEG_MA, D_MODEL)
    mb_ref[...] = proj(_SEG_MB, D_MODEL)


def _inproj(x, pos_block_of_tile, cos_tab, sin_tab, p, seq_tiles=None):
    n = x.shape[0]
    tm = TOKEN_TILE
    nt = n // tm
    row = lambda i: (i, 0)
    const = lambda i: (0, 0)
    widths = (A_WIDTH, A_WIDTH, A_WIDTH, G_KEY_WIDTH, G_KEY_WIDTH, G_VAL_WIDTH, G_KEY_WIDTH,
              G_VAL_WIDTH, D_MODEL, D_MODEL)
    out_shape = [jax.ShapeDtypeStruct((n, w), F32) for w in widths]
    out_specs = [pl.BlockSpec((tm, w), row) for w in widths]
    out_shape.append(jax.ShapeDtypeStruct((nt, 1, A_WIDTH), F32))
    out_specs.append(pl.BlockSpec((1, 1, A_WIDTH), lambda i: (i, 0, 0)))
    if seq_tiles is not None:
        for _ in range(2):
            out_shape.append(jax.ShapeDtypeStruct((nt // seq_tiles, A_WIDTH, seq_tiles * tm), F32))
            out_specs.append(pl.BlockSpec((1, A_WIDTH, tm), lambda i: (i // seq_tiles, 0, i % seq_tiles)))
    tab_spec =pl.BlockSpec((tm, A_WIDTH), lambda i: (pos_block_of_tile(i), 0))
    return pl.pallas_call(
        _inproj_kernel,
        grid=(nt,),
        in_specs=[
            pl.BlockSpec((tm, D_MODEL), row),
            pl.BlockSpec((1, D_MODEL), const),
            pl.BlockSpec((D_MODEL, _W_IN_PADDED), const),
            tab_spec, tab_spec,
            pl.BlockSpec((1, A_WIDTH), const),
            pl.BlockSpec((1, A_WIDTH), const),
            pl.BlockSpec((A_WIDTH, A_WIDTH), const),
            pl.BlockSpec((GLR_PAD, G_KEY_WIDTH), const),
            pl.BlockSpec((1, G_KEY_WIDTH), const),
        ],
        out_specs=out_specs,
        out_shape=out_shape,
        compiler_params=pltpu.CompilerParams(dimension_semantics=("arbitrary",), vmem_limit_bytes=VMEM_LIMIT),
        name="inproj",
    )(x, p["g1"], p["w_in"], cos_tab, sin_tab, p["qn"], p["kn"], p["bd"], p["w2"], p["gb"])


def _rope_tables(pos):
    half = A_HEAD_DIM // 2
    inv_freq = ROPE_THETA ** (-jnp.arange(half, dtype=F32) / half)
    ang = pos.astype(F32)[:, None] * inv_freq[None, :]
    reps = A_WIDTH // half
    return jnp.tile(jnp.cos(ang), (1, reps)), jnp.tile(jnp.sin(ang), (1, reps))


def _prep_mixer_params(norm1_g, w_in, a_q_norm, a_k_norm, g_gate_w2, g_gate_b):
    glr_off = 3 * A_WIDTH + 2 * G_KEY_WIDTH + G_VAL_WIDTH
    w_pad = jnp.concatenate([
        w_in[:, :glr_off],
        jnp.pad(w_in[:, glr_off:glr_off + G_GATE_RANK], ((0, 0), (0, GLR_PAD - G_GATE_RANK))),
        w_in[:, glr_off + G_GATE_RANK:]], axis=1).astype(BF16)
    head_id = jnp.arange(A_WIDTH) // A_HEAD_DIM
    return {
        "g1": norm1_g.reshape(1, D_MODEL),
        "w_in": w_pad,
        "qn": jnp.tile(a_q_norm, A_HEADS).reshape(1, A_WIDTH),
        "kn": jnp.tile(a_k_norm, A_HEADS).reshape(1, A_WIDTH),
        "bd": (head_id[:, None] == head_id[None, :]).astype(BF16),
        "w2": jnp.pad(g_gate_w2, ((0, GLR_PAD - G_GATE_RANK), (0, 0))).astype(BF16),
        "gb": g_gate_b.reshape(1, G_KEY_WIDTH),
    }


def _topk_rows_mask(s, valid, n_sel):
    n = s.shape[0]
    sm = jnp.where(valid, s, -jnp.inf)
    ridx = lax.broadcasted_iota(jnp.int32, s.shape, 0)
    cnt = jnp.zeros(s.shape, jnp.int32)
    for r in range(n):
        row = sm[r:r + 1, :]
        beats = (row > sm) | ((row == sm) & (r < ridx))
        cnt = cnt + beats.astype(jnp.int32)
    return valid & (cnt < n_sel)


PROMPT_BLOCKS_PER_ITER = 4


def _moba_prompt_kernel(q_ref, k_ref, v_ref, km_ref, o_ref, sel_ref):
    ob = pl.program_id(2)
    blk = MOBA_BLOCK
    nb = km_ref.shape[0]
    scale = A_HEAD_DIM ** -0.5
    kpos = lax.broadcasted_iota(jnp.int32, (blk, blk), 0)
    qpos = lax.broadcasted_iota(jnp.int32, (blk, blk), 1)
    causal = kpos <= qpos
    n_heads = LANES // A_HEAD_DIM
    head_lanes = [slice(hh * A_HEAD_DIM, (hh + 1) * A_HEAD_DIM) for hh in range(n_heads)]
    nidx = lax.broadcasted_iota(jnp.int32, (nb, blk), 0)
    qs = []
    for hh, sl in enumerate(head_lanes):
        q = q_ref[:, sl]
        s_sel = lax.dot_general(km_ref[:, sl], q, (((1,), (1,)), ((), ())),
                                precision=lax.Precision.HIGHEST, preferred_element_type=F32)
        sel_ref[hh] = _topk_rows_mask(s_sel, nidx < ob, MOBA_TOPK).astype(F32)
        qs.append((q * scale).astype(BF16))

    def scores(n, hh):
        kb = k_ref[pl.ds(n * blk, blk), head_lanes[hh]].astype(BF16)
        return _nt(kb, qs[hh])

    def values_t(n, hh):
        return v_ref[pl.ds(n * blk, blk), head_lanes[hh]].T.astype(BF16)

    init = []
    for hh in range(n_heads):
        s0 = jnp.where(causal, scores(ob, hh), -jnp.inf)
        m0 = jnp.max(s0, axis=0, keepdims=True)
        p0 = jnp.exp(s0 - m0)
        init.append((m0, jnp.sum(p0, axis=0, keepdims=True), _dot(values_t(ob, hh), p0.astype(BF16))))

    def body(it, carry):
        new = []
        for hh in range(n_heads):
            m, l, acc = carry[hh]
            blocks = [jnp.minimum(it * PROMPT_BLOCKS_PER_ITER + j, nb - 1)
                      for j in range(PROMPT_BLOCKS_PER_ITER)]
            ss = [jnp.where(sel_ref[hh, pl.ds(n, 1), :] > 0.5, scores(n, hh), -jnp.inf) for n in blocks]
            m_new = m
            for s in ss:
                m_new = jnp.maximum(m_new, jnp.max(s, axis=0, keepdims=True))
            alpha = jnp.exp(m - m_new)
            l = alpha * l
            acc = alpha * acc
            for n, s in zip(blocks, ss):
                p = jnp.exp(s - m_new)
                l = l + jnp.sum(p, axis=0, keepdims=True)
                acc = acc + _dot(values_t(n, hh), p.astype(BF16))
            new.append((m_new, l, acc))
        return tuple(new)

    n_iter = (ob + PROMPT_BLOCKS_PER_ITER - 1) // PROMPT_BLOCKS_PER_ITER
    final = lax.fori_loop(0, n_iter, body, tuple(init))
    o_ref[...] = jnp.concatenate([(acc / l).T for _, l, acc in final], axis=1)


def _moba_prompt(q, k, v, kmeans, batch, seq):
    nb = seq // MOBA_BLOCK
    hp = A_WIDTH // LANES
    return pl.pallas_call(
        _moba_prompt_kernel,
        grid=(batch, hp, nb),
        in_specs=[
            pl.BlockSpec((MOBA_BLOCK, LANES), lambda b, h, i: (b * nb + i, h)),
            pl.BlockSpec((seq, LANES), lambda b, h, i: (b, h)),
            pl.BlockSpec((seq, LANES), lambda b, h, i: (b, h)),
            pl.BlockSpec((nb, LANES), lambda b, h, i: (b, h)),
        ],
        out_specs=pl.BlockSpec((MOBA_BLOCK, LANES), lambda b, h, i: (b * nb + i, h)),
        out_shape=jax.ShapeDtypeStruct((batch * seq, A_WIDTH), F32),
        scratch_shapes=[pltpu.VMEM((LANES // A_HEAD_DIM, nb, MOBA_BLOCK), F32)],
        compiler_params=pltpu.CompilerParams(
            dimension_semantics=("arbitrary", "arbitrary", "arbitrary"), vmem_limit_bytes=VMEM_LIMIT),
        name="moba_prompt",
    )(q, k, v, kmeans)


def _cumsum_rows(tri, g):
    hi, mid, lo = _split3(g)
    return _dot(tri, hi) + _dot(tri, mid) + _dot(tri, lo)


GLA_TILE = 128


def _gla_prompt_kernel(q_ref, k_ref, v_ref, g_ref, tri_ref, o_ref, s_ref, st_ref):
    t = pl.program_id(0)
    c, sub = G_CHUNK, G_SUB
    batch = q_ref.shape[0]

    @pl.when(t == 0)
    def _():
        st_ref[...] = jnp.zeros_like(st_ref)

    tri = tri_ref[...]
    key_head = lax.broadcasted_iota(jnp.int32, (1, G_KEY_WIDTH), 1) // G_DK
    blockdiag = (lax.broadcasted_iota(jnp.int32, (G_VAL_WIDTH, G_KEY_WIDTH), 0) // G_DV
                 == lax.broadcasted_iota(jnp.int32, (G_VAL_WIDTH, G_KEY_WIDTH), 1) // G_DK)
    for ci, bi in [(ci, bi) for ci in range(GLA_TILE // c) for bi in range(batch)]:
        rows = slice(ci * c, (ci + 1) * c)
        q, k, v, g = q_ref[bi, rows, :], k_ref[bi, rows, :], v_ref[bi, rows, :], g_ref[bi, rows, :]
        b = _cumsum_rows(tri, g)
        blast = b[c - 1:c, :]
        q_inter = (q * jnp.exp(b)).astype(BF16)
        k_dec = (k * jnp.exp(blast - b)).astype(BF16)
        vb = v.astype(BF16)
        st = st_ref[bi]
        o_rows = []
        for i in range(c // sub):
            bref = b[i * sub - 1:i * sub, :] if i > 0 else jnp.zeros((1, G_KEY_WIDTH), F32)
            r = slice(i * sub, (i + 1) * sub)
            ncol = (i + 1) * sub
            q_sub = q[r, :] * jnp.exp(b[r, :] - bref)
            k_sub = (k[:ncol, :] * jnp.exp(bref - b[:ncol, :])).astype(BF16)
            q_stack = jnp.concatenate(
                [jnp.where(key_head == h, q_sub, 0.0) for h in range(G_HEADS)], axis=0).astype(BF16)
            a = _nt(q_stack, k_sub)
            rr = lax.broadcasted_iota(jnp.int32, (G_HEADS * sub, ncol), 0) % sub + i * sub
            cc = lax.broadcasted_iota(jnp.int32, (G_HEADS * sub, ncol), 1)
            av = _dot(jnp.where(cc <= rr, a, 0.0).astype(BF16), vb[:ncol, :])
            o_rows.append(jnp.concatenate(
                [av[h * sub:(h + 1) * sub, h * G_DV:(h + 1) * G_DV] for h in range(G_HEADS)], axis=1))
        o_ref[bi, rows, :] = _nt(q_inter, st.astype(BF16)) + jnp.concatenate(o_rows, axis=0)
        st_ref[bi] = st * jnp.exp(blast) + jnp.where(blockdiag, _tn(vb, k_dec), 0.0)

    @pl.when(t == pl.num_programs(0) - 1)
    def _():
        for bi in range(batch):
            for h in range(G_HEADS):
                s_ref[bi, h] = st_ref[bi, h * G_DV:(h + 1) * G_DV, h * G_DK:(h + 1) * G_DK].T


def _gla_prompt(gq, gk, gv, logg, batch, seq):
    tile = lambda w: pl.BlockSpec((batch, GLA_TILE, w), lambda t: (0, t, 0))
    tri = jnp.tril(jnp.ones((G_CHUNK, G_CHUNK), F32)).astype(BF16)
    shaped = lambda a: a.reshape(batch, seq, a.shape[-1])
    state_spec = pl.BlockSpec((batch, G_HEADS, G_DK, G_DV), lambda t: (0, 0, 0, 0))
    o, st = pl.pallas_call(
        _gla_prompt_kernel,
        grid=(seq // GLA_TILE,),
        in_specs=[tile(G_KEY_WIDTH), tile(G_KEY_WIDTH), tile(G_VAL_WIDTH), tile(G_KEY_WIDTH),
                  pl.BlockSpec((G_CHUNK, G_CHUNK), lambda t: (0, 0))],
        out_specs=[tile(G_VAL_WIDTH), state_spec],
        out_shape=[
            jax.ShapeDtypeStruct((batch, seq, G_VAL_WIDTH), F32),
            jax.ShapeDtypeStruct((batch, G_HEADS, G_DK, G_DV), F32),
        ],
        scratch_shapes=[pltpu.VMEM((batch, G_VAL_WIDTH, G_KEY_WIDTH), F32)],
        compiler_params=pltpu.CompilerParams(dimension_semantics=("arbitrary",), vmem_limit_bytes=VMEM_LIMIT),
        name="gla_prompt",
    )(shaped(gq), shaped(gk), shaped(gv), shaped(logg), tri)
    return o.reshape(batch * seq, G_VAL_WIDTH), st


GLA_SAMPLE_SEQS = 8


def _gla_sample_kernel(q_ref, k_ref, v_ref, g_ref, tri_ref, s0_ref, o_ref, s_ref, *, steps):
    q, k, v, g = q_ref[...], k_ref[...], v_ref[...], g_ref[...]
    b = _cumsum_rows(tri_ref[...], g)
    eb = jnp.exp(b)
    qd = (q * eb).astype(BF16)
    kinv = (k * jnp.exp(-b)).astype(BF16)
    vb = v.astype(BF16)
    rr = lax.broadcasted_iota(jnp.int32, (steps, steps), 0)
    cc = lax.broadcasted_iota(jnp.int32, (steps, steps), 1)
    eye = lax.broadcasted_iota(jnp.int32, (G_DK, G_DK), 0) == lax.broadcasted_iota(jnp.int32, (G_DK, G_DK), 1)
    out_rows = []
    for s in range(GLA_SAMPLE_SEQS):
        rows = slice(s * steps, (s + 1) * steps)
        blast = b[(s + 1) * steps - 1:(s + 1) * steps, :]
        k_dec = (k[rows, :] * jnp.exp(blast - b[rows, :])).astype(BF16)
        outs = []
        for h in range(G_HEADS):
            hs = slice(h * G_DK, (h + 1) * G_DK)
            vs = slice(h * G_DV, (h + 1) * G_DV)
            s0 = s0_ref[s, h]
            a = jnp.where(cc <= rr, _nt(qd[rows, hs], kinv[rows, hs]), 0.0).astype(BF16)
            outs.append(_dot(qd[rows, hs], s0.astype(BF16)) + _dot(a, vb[rows, vs]))
            dcol = jnp.sum(jnp.where(eye, jnp.exp(blast[:, hs]), 0.0), axis=1, keepdims=True)
            s_ref[s, h] = dcol * s0 + _tn(k_dec[:, hs], vb[rows, vs])
        out_rows.append(jnp.concatenate(outs, axis=1))
    o_ref[...] = jnp.concatenate(out_rows, axis=0)


def _gla_sample(gq, gk, gv, logg, state, steps):
    nseq = state.shape[0]
    sb = GLA_SAMPLE_SEQS
    rows = sb * steps
    idx = jnp.arange(rows)
    tri = ((idx[:, None] >= idx[None, :]) & (idx[:, None] // steps == idx[None, :] // steps)).astype(BF16)
    row = lambda i: (i, 0)
    return pl.pallas_call(
        functools.partial(_gla_sample_kernel, steps=steps),
        grid=(nseq // sb,),
        in_specs=[
            pl.BlockSpec((rows, G_KEY_WIDTH), row),
            pl.BlockSpec((rows, G_KEY_WIDTH), row),
            pl.BlockSpec((rows, G_VAL_WIDTH), row),
            pl.BlockSpec((rows, G_KEY_WIDTH), row),
            pl.BlockSpec((rows, rows), lambda i: (0, 0)),
            pl.BlockSpec((sb, G_HEADS, G_DK, G_DV), lambda i: (i, 0, 0, 0)),
        ],
        out_specs=[
            pl.BlockSpec((rows, G_VAL_WIDTH), row),
            pl.BlockSpec((sb, G_HEADS, G_DK, G_DV), lambda i: (i, 0, 0, 0)),
        ],
        out_shape=[
            jax.ShapeDtypeStruct((nseq * steps, G_VAL_WIDTH), F32),
            jax.ShapeDtypeStruct(state.shape, F32),
        ],
        compiler_params=pltpu.CompilerParams(dimension_semantics=("arbitrary",), vmem_limit_bytes=VMEM_LIMIT),
        name="gla_sample",
    )(gq, gk, gv, logg, tri, state)


PAGES_PER_BLOCK = MOBA_BLOCK // PAGE_SIZE


SAMPLE_PAGES_PER_STEP = 16


def _moba_sample_kernel(pt_ref, q_ref, kn_ref, vn_ref, *refs, steps, n_blocks):
    del pt_ref
    pps = SAMPLE_PAGES_PER_STEP
    k_refs, v_refs = refs[:pps], refs[pps:2 * pps]
    o_ref, wq_ref, m_ref, l_ref, ss_ref, acc_ref = refs[2 * pps:]
    g = pl.program_id(1)
    nq = A_HEADS * steps
    scale = A_HEAD_DIM ** -0.5

    @pl.when(g == 0)
    def _():
        qt = jnp.concatenate([q_ref[...]] * A_HEADS, axis=0)
        rh = lax.broadcasted_iota(jnp.int32, (nq, A_WIDTH), 0) // steps
        ch = lax.broadcasted_iota(jnp.int32, (nq, A_WIDTH), 1) // A_HEAD_DIM
        wq_ref[...] = jnp.where(rh == ch, qt * scale, 0.0).astype(BF16)

    wq = wq_ref[...]
    for bi in range(pps // PAGES_PER_BLOCK):
        n = g * (pps // PAGES_PER_BLOCK) + bi
        pages = range(bi * PAGES_PER_BLOCK, (bi + 1) * PAGES_PER_BLOCK)
        st = jnp.concatenate([_dot(wq, k_refs[j][0].astype(BF16)).T for j in pages], axis=0)
        m = jnp.max(st, axis=0, keepdims=True)
        p = jnp.exp(st - m)
        m_ref[pl.ds(n, 1), :] = m
        l_ref[pl.ds(n, 1), :] = jnp.sum(p, axis=0, keepdims=True)
        ss_ref[pl.ds(n, 1), :] = jnp.sum(st, axis=0, keepdims=True)
        pb = p.astype(BF16)
        acc = None
        for jj, j in enumerate(pages):
            part = _dot(v_refs[j][0].astype(BF16), pb[jj * PAGE_SIZE:(jj + 1) * PAGE_SIZE, :])
            acc = part if acc is None else acc + part
        acc_ref[n] = acc

    @pl.when(g == pl.num_programs(1) - 1)
    def _():
        ss = ss_ref[...]
        sel = _topk_rows_mask(ss, jnp.ones(ss.shape, jnp.bool_), min(MOBA_TOPK, n_blocks))
        s_own = _nt(kn_ref[...].astype(BF16), wq)
        tk = lax.broadcasted_iota(jnp.int32, (steps, nq), 0)
        tq = lax.broadcasted_iota(jnp.int32, (steps, nq), 1) % steps
        s_own = jnp.where(tk <= tq, s_own, -jnp.inf)
        mm = jnp.where(sel, m_ref[...], -jnp.inf)
        big = jnp.maximum(jnp.max(mm, axis=0, keepdims=True), jnp.max(s_own, axis=0, keepdims=True))
        w = jnp.where(sel, jnp.exp(mm - big), 0.0)
        p_own = jnp.exp(s_own - big)
        denom = jnp.sum(w * l_ref[...], axis=0, keepdims=True) + jnp.sum(p_own, axis=0, keepdims=True)
        acc = _tn(vn_ref[...].astype(BF16), p_own.astype(BF16))
        for j in range(n_blocks):
            acc = acc + w[j:j + 1, :] * acc_ref[j]
        out = (acc / denom).T
        o_ref[...] = jnp.concatenate(
            [out[h * steps:(h + 1) * steps, h * A_HEAD_DIM:(h + 1) * A_HEAD_DIM] for h in range(A_HEADS)], axis=1)


def _moba_sample(q, k_new, v_new, cache_kt, cache_vt, page_table, steps):
    nseq, n_pages = page_table.shape
    n_blocks = n_pages // PAGES_PER_BLOCK
    pps = SAMPLE_PAGES_PER_STEP
    nq = A_HEADS * steps
    tok = pl.BlockSpec((steps, A_WIDTH), lambda s, g, pt: (s, 0))

    def page(j):
        return pl.BlockSpec((1, A_WIDTH, PAGE_SIZE), lambda s, g, pt: (pt[s * n_pages + g * pps + j], 0, 0))

    pages = [page(j) for j in range(pps)]
    grid_spec = pltpu.PrefetchScalarGridSpec(
        num_scalar_prefetch=1,
        grid=(nseq, n_pages // pps),
        in_specs=[tok, tok, tok] + pages + pages,
        out_specs=tok,
        scratch_shapes=[
            pltpu.VMEM((nq, A_WIDTH), BF16),
            pltpu.VMEM((n_blocks, nq), F32),
            pltpu.VMEM((n_blocks, nq), F32),
            pltpu.VMEM((n_blocks, nq), F32),
            pltpu.VMEM((n_blocks, A_WIDTH, nq), F32),
        ],
    )
    return pl.pallas_call(
        functools.partial(_moba_sample_kernel, steps=steps, n_blocks=n_blocks),
        grid_spec=grid_spec,
        out_shape=jax.ShapeDtypeStruct((nseq * steps, A_WIDTH), F32),
        compiler_params=pltpu.CompilerParams(
            dimension_semantics=("arbitrary", "arbitrary"), vmem_limit_bytes=VMEM_LIMIT),
        name="moba_sample",
    )(page_table.reshape(-1), q, k_new, v_new, *([cache_kt] * pps), *([cache_vt] * pps))


def _merge_kernel(x_ref, oa_ref, ob_ref, go_ref, ma_ref, mb_ref, gn_ref, wa_ref, wb_ref, wo_ref, h_ref):
    ya = _dot(oa_ref[...].astype(BF16), wa_ref[...])
    ob = ob_ref[...]
    normed = []
    for h in range(G_HEADS):
        seg = ob[:, h * G_DV:(h + 1) * G_DV]
        normed.append(seg * lax.rsqrt(jnp.mean(seg * seg, axis=-1, keepdims=True) + EPS))
    go = go_ref[...]
    ob = jnp.concatenate(normed, axis=1) * gn_ref[...] * (go * jax.nn.sigmoid(go))
    yb = _dot(ob.astype(BF16), wb_ref[...])
    merged = jax.nn.sigmoid(ma_ref[...]) * ya + jax.nn.sigmoid(mb_ref[...]) * yb
    h_ref[...] = x_ref[...] + _dot(merged.astype(BF16), wo_ref[...])


def _merge(x, o_a, o_b, gout, ma, mb, p):
    n = x.shape[0]
    tm = TOKEN_TILE
    row = lambda i: (i, 0)
    const = lambda i: (0, 0)
    return pl.pallas_call(
        _merge_kernel,
        grid=(n // tm,),
        in_specs=[
            pl.BlockSpec((tm, D_MODEL), row),
            pl.BlockSpec((tm, A_WIDTH), row),
            pl.BlockSpec((tm, G_VAL_WIDTH), row),
            pl.BlockSpec((tm, G_VAL_WIDTH), row),
            pl.BlockSpec((tm, D_MODEL), row),
            pl.BlockSpec((tm, D_MODEL), row),
            pl.BlockSpec((1, G_VAL_WIDTH), const),
            pl.BlockSpec((A_WIDTH, D_MODEL), const),
            pl.BlockSpec((G_VAL_WIDTH, D_MODEL), const),
            pl.BlockSpec((D_MODEL, D_MODEL), const),
        ],
        out_specs=pl.BlockSpec((tm, D_MODEL), row),
        out_shape=jax.ShapeDtypeStruct((n, D_MODEL), F32),
        compiler_params=pltpu.CompilerParams(dimension_semantics=("arbitrary",), vmem_limit_bytes=VMEM_LIMIT),
        name="merge",
    )(x, o_a, o_b, gout, ma, mb, p["gn"], p["wa"], p["wb"], p["wo"])


PEER_TOKEN_TILE = 512
PEER_KEYS_PER_STEP = 8
PEER_EXPERT_TILE = PEER_KEYS_PER_STEP * P_N_KEYS
PEER_GATE_GROUP = 4
PEER_GATE_KEY_SPLIT = 2
BIG = 1e30


def _top_rows(s, count):
    cur = s
    vals = []
    for _ in range(count):
        m = jnp.max(cur, axis=0, keepdims=True)
        vals.append(m)
        cur = jnp.where(cur >= m, -jnp.inf, cur)
    return vals


def _peer_kernel(h_ref, g2_ref, wq_ref, sk1_ref, sk2_ref, u_ref, vt_ref, o_ref,
                 hb_ref, th_ref, a_ref, s2_ref, c_ref, acc_ref, act_ref, p_ref, ths_ref, as_ref):
    e = pl.program_id(1)
    tm = h_ref.shape[0]

    @pl.when(e == 0)
    def _():
        hx = h_ref[...]
        hn = hx * lax.rsqrt(jnp.mean(hx * hx, axis=-1, keepdims=True) + EPS) * g2_ref[...]
        hb = hn.astype(BF16)
        hb_ref[...] = hb
        for hd in range(P_HEADS):
            qt = _nt(wq_ref[hd * 2 * P_HALF:(hd + 1) * 2 * P_HALF, :], hb)
            s1 = _dot(sk1_ref[hd], qt[:P_HALF].astype(BF16))
            s2 = _dot(sk2_ref[hd], qt[P_HALF:].astype(BF16))
            v1 = _top_rows(s1, P_TOPK)
            v2 = _top_rows(s2, P_TOPK)
            cand = [v1[a] + v2[b] for a in range(P_TOPK) for b in range(P_TOPK // (a + 1))]
            pad = -len(cand) % 8
            cand = jnp.concatenate(cand + [jnp.full((pad, tm), -jnp.inf, F32)], axis=0)
            tops = _top_rows(cand, P_TOPK + 1)
            cut = 0.5 * (tops[P_TOPK - 1] + tops[P_TOPK])
            z = jnp.sum(jnp.where(cand >= tops[P_TOPK - 1], jnp.exp(cand - tops[0]), 0.0),
                        axis=0, keepdims=True)
            th = jnp.where(s1 >= v1[P_TOPK - 1], cut - s1, BIG)
            aa = jnp.exp(s1 - v1[0]) / z
            s2m = jnp.where(s2 >= v2[P_TOPK - 1], s2, -BIG)
            cc2 = jnp.exp(s2 - v2[0])
            for cc in range(tm // LANES):
                cols = slice(cc * LANES, (cc + 1) * LANES)
                th_ref[hd, cc] = th[:, cols]
                a_ref[hd, cc] = aa[:, cols]
                s2_ref[hd, cc] = s2m[:, cols]
                c_ref[hd, cc] = cc2[:, cols]
        acc_ref[...] = jnp.zeros_like(acc_ref)

    n_cc = tm // LANES
    group = PEER_GATE_GROUP
    half = P_N_KEYS // PEER_GATE_KEY_SPLIT

    base = pl.multiple_of(e * PEER_KEYS_PER_STEP, PEER_KEYS_PER_STEP)
    ths_ref[...] = th_ref[:, :, pl.ds(base, PEER_KEYS_PER_STEP), :]
    as_ref[...] = a_ref[:, :, pl.ds(base, PEER_KEYS_PER_STEP), :]

    def gate_unit(grp, cc, act_r, p_w):
        cols = slice(cc * LANES, (cc + 1) * LANES)
        for jh in range(PEER_GATE_KEY_SPLIT):
            jr = slice(jh * half, (jh + 1) * half)
            g = [jnp.zeros((half, LANES), F32) for _ in range(group)]
            for hd in range(P_HEADS):
                s2h = s2_ref[hd, cc, jr, :]
                ch = c_ref[hd, cc, jr, :]
                for k in range(group):
                    i = grp * group + k
                    th = ths_ref[hd, cc, i:i + 1, :]
                    aa = as_ref[hd, cc, i:i + 1, :]
                    g[k] = g[k] + jnp.where(s2h >= th, ch, 0.0) * aa
            for k in range(group):
                r0 = (grp * group + k) * P_N_KEYS + jh * half
                act = act_r[r0:r0 + half, cols]
                gl = 0.5 * act * (1.0 + lax.erf(act * (1.0 / math.sqrt(2.0))))
                p_w[r0:r0 + half, cols] = (g[k] * gl).astype(BF16)

    act_ref[...] = _nt(u_ref[...], hb_ref[...])
    for grp in range(PEER_KEYS_PER_STEP // group):
        for cc in range(n_cc):
            gate_unit(grp, cc, act_ref, p_ref)
    acc_ref[...] += _dot(vt_ref[0], p_ref[...])

    @pl.when(e == pl.num_programs(1) - 1)
    def _():
        o_ref[...] = h_ref[...] + acc_ref[...].T


def _peer(h, p):
    n = h.shape[0]
    tm = PEER_TOKEN_TILE
    te = PEER_EXPERT_TILE
    n_et = p["u"].shape[0] // te
    row = lambda t, e: (t, 0)
    const2 = lambda t, e: (0, 0)
    const3 = lambda t, e: (0, 0, 0)
    sel_scratch = pltpu.VMEM((P_HEADS, tm // LANES, P_N_KEYS, LANES), F32)
    return pl.pallas_call(
        _peer_kernel,
        grid=(n // tm, n_et),
        in_specs=[
            pl.BlockSpec((tm, D_MODEL), row),
            pl.BlockSpec((1, D_MODEL), const2),
            pl.BlockSpec((P_HEADS * 2 * P_HALF, D_MODEL), const2),
            pl.BlockSpec((P_HEADS, P_N_KEYS, P_HALF), const3),
            pl.BlockSpec((P_HEADS, P_N_KEYS, P_HALF), const3),
            pl.BlockSpec((te, D_MODEL), lambda t, e: (e, 0)),
            pl.BlockSpec((1, D_MODEL, te), lambda t, e: (e, 0, 0)),
        ],
        out_specs=pl.BlockSpec((tm, D_MODEL), row),
        out_shape=jax.ShapeDtypeStruct((n, D_MODEL), F32),
        scratch_shapes=[
            pltpu.VMEM((tm, D_MODEL), BF16),
            sel_scratch, sel_scratch, sel_scratch, sel_scratch,
            pltpu.VMEM((D_MODEL, tm), F32),
            pltpu.VMEM((te, tm), F32),
            pltpu.VMEM((te, tm), BF16),
            pltpu.VMEM((P_HEADS, tm // LANES, PEER_KEYS_PER_STEP, LANES), F32),
            pltpu.VMEM((P_HEADS, tm // LANES, PEER_KEYS_PER_STEP, LANES), F32),
        ],
        compiler_params=pltpu.CompilerParams(
            dimension_semantics=("arbitrary", "arbitrary"), vmem_limit_bytes=VMEM_LIMIT),
        name="peer",
    )(h, p["g2"], p["wq_t"], p["sk1"], p["sk2"], p["u"], p["v_t"])


def _prep_post_params(g_out_norm, w_branch_a, w_branch_b, w_out, norm2_g, peer_wq, sk1, sk2, peer_u, peer_v):
    return {
        "gn": jnp.tile(g_out_norm, G_HEADS).reshape(1, G_VAL_WIDTH),
        "wa": w_branch_a.astype(BF16),
        "wb": w_branch_b.astype(BF16),
        "wo": w_out.astype(BF16),
        "g2": norm2_g.reshape(1, D_MODEL),
        "wq_t": peer_wq.T.astype(BF16),
        "sk1": sk1.astype(BF16),
        "sk2": sk2.astype(BF16),
        "u": peer_u.astype(BF16),
        "v_t": jnp.transpose(peer_v.reshape(-1, PEER_EXPERT_TILE, D_MODEL), (0, 2, 1)).astype(BF16),
    }


def kernel(x_prompt, x_sample, cache_k, cache_v, state_gla, page_table, norm1_g, w_in, a_q_norm, a_k_norm,
           g_gate_w2, g_gate_b, g_out_norm, w_branch_a, w_branch_b, w_out, norm2_g, peer_wq,
           peer_subkeys1, peer_subkeys2, peer_u, peer_v):
    batch, seq, d = x_prompt.shape
    nseq, steps, _ = x_sample.shape
    depth, n_phys = cache_k.shape[:2]
    assert depth == 1 and d == D_MODEL
    assert seq % MOBA_BLOCK == 0 and MOBA_BLOCK == TOKEN_TILE
    past_len = page_table.shape[1] * PAGE_SIZE
    assert past_len % MOBA_BLOCK == 0 and TOKEN_TILE % steps == 0 and (nseq * steps) % TOKEN_TILE == 0
    assert math.gcd(steps, G_CHUNK) == steps and nseq % GLA_SAMPLE_SEQS == 0

    mp = _prep_mixer_params(norm1_g[0], w_in[0], a_q_norm[0], a_k_norm[0], g_gate_w2[0], g_gate_b[0])
    pp = _prep_post_params(g_out_norm[0], w_branch_a[0], w_branch_b[0], w_out[0], norm2_g[0], peer_wq[0],
                           peer_subkeys1[0], peer_subkeys2[0], peer_u[0], peer_v[0])

    nb = seq // TOKEN_TILE
    cos_p, sin_p = _rope_tables(jnp.arange(seq, dtype=jnp.int32))
    xp = x_prompt.reshape(batch * seq, d)
    q, k, v, gq, gk, gv, lg, go, ma, mb, km, k_t, v_t = _inproj(
        xp, lambda i: i % nb, cos_p, sin_p, mp, seq_tiles=nb)
    o_a = _moba_prompt(q, k, v, km.reshape(batch * nb, A_WIDTH), batch, seq)
    o_b, st_p = _gla_prompt(gq, gk, gv, lg, batch, seq)
    y_p = _peer(_merge(xp, o_a, o_b, go, ma, mb, pp), pp)

    pos_s = past_len + jnp.arange(TOKEN_TILE, dtype=jnp.int32) % steps
    cos_s, sin_s = _rope_tables(pos_s)
    xs = x_sample.reshape(nseq * steps, d)
    qs, ks, vs, gqs, gks, gvs, lgs, gos, mas, mbs, _ = _inproj(xs, lambda i: 0, cos_s, sin_s, mp)
    def pages_t(c):
        c = c.reshape(n_phys, PAGE_SIZE, A_HEADS, A_HEAD_DIM)
        return jnp.transpose(c, (0, 2, 3, 1)).reshape(n_phys, A_WIDTH, PAGE_SIZE)

    o_as = _moba_sample(qs, ks, vs, pages_t(cache_k), pages_t(cache_v), page_table, steps)
    o_bs, st_s = _gla_sample(gqs, gks, gvs, lgs, state_gla.reshape(nseq, G_HEADS, G_DK, G_DV), steps)
    y_s = _peer(_merge(xs, o_as, o_bs, gos, mas, mbs, pp), pp)

    def cache_view(a_t):
        return jnp.transpose(a_t.reshape(1, batch, A_HEADS, A_HEAD_DIM, seq), (0, 1, 4, 2, 3))

    kv_s = (1, nseq, steps, A_HEADS, A_HEAD_DIM)
    return (y_p.reshape(batch, seq, d), y_s.reshape(nseq, steps, d),
            cache_view(k_t), cache_view(v_t), st_p[None],
            ks.reshape(kv_s), vs.reshape(kv_s), st_s[None])
```

```python
import functools
import math

import jax
import jax.numpy as jnp
from jax import lax
from jax.experimental import pallas as pl
from jax.experimental.pallas import tpu as pltpu

F32 = jnp.float32
BF16 = jnp.bfloat16

D_MODEL = 1024
PAST_LEN = 8192
PAGE_SIZE = 128
A_HEADS = 8
A_HEAD_DIM = 64
A_WIDTH = A_HEADS * A_HEAD_DIM
MOBA_BLOCK = 256
MOBA_TOPK = 3
ROPE_THETA = 10000.0
G_HEADS = 4
G_DK = 64
G_DV = 128
G_KEY_WIDTH = G_HEADS * G_DK
G_VAL_WIDTH = G_HEADS * G_DV
G_GATE_RANK = 16
G_GATE_NORMALIZER = 16.0
G_CHUNK = 64
G_SUB = 16
P_HEADS = 8
P_N_KEYS = 128
P_HALF = 128
P_TOPK = 16
EPS = 1e-6

LANES = 128
SUBLANES = 8
VMEM_LIMIT = 56 * 1024 * 1024

TOKEN_TILE = 256
GLR_PAD = LANES

NEG = -1e30


def _nt(a, b):
    return lax.dot_general(a, b, (((1,), (1,)), ((), ())), preferred_element_type=F32)


def _tn(a, b):
    return lax.dot_general(a, b, (((0,), (0,)), ((), ())), preferred_element_type=F32)


def _dot(a, b):
    return jnp.dot(a, b, preferred_element_type=F32)


def _split2(x):
    hi = x.astype(BF16)
    lo = (x - hi.astype(F32)).astype(BF16)
    return hi, lo


def _split3(x):
    hi = x.astype(BF16)
    r = x - hi.astype(F32)
    mid = r.astype(BF16)
    lo = (r - mid.astype(F32)).astype(BF16)
    return hi, mid, lo


_SEG_Q, _SEG_K, _SEG_V = 0, 512, 1024
_SEG_GQ, _SEG_GK, _SEG_GV = 1536, 1792, 2048
_SEG_GLR = 2560
_SEG_GOUT = _SEG_GLR + GLR_PAD
_SEG_MA = _SEG_GOUT + 512
_SEG_MB = _SEG_MA + 1024
_W_IN_PADDED = _SEG_MB + 1024


def _inproj_kernel(x_ref, g1_ref, w_ref, cos_ref, sin_ref, qn_ref, kn_ref, bd_ref, w2_ref, gb_ref,
                   q_ref, k_ref, v_ref, gq_ref, gk_ref, gv_ref, lg_ref, go_ref, ma_ref, mb_ref, km_ref,
                   *kv_t_refs):
    x = x_ref[...]
    xn = x * lax.rsqrt(jnp.mean(x * x, axis=-1, keepdims=True) + EPS) * g1_ref[...]
    xb = xn.astype(BF16)

    def proj(off, width):
        return _dot(xb, w_ref[:, off:off + width])

    cos = cos_ref[...]
    sin = sin_ref[...]
    lane = lax.broadcasted_iota(jnp.int32, (1, A_WIDTH), 1)
    first_half = (lane % A_HEAD_DIM) < (A_HEAD_DIM // 2)
    bd = bd_ref[...]

    def qk_epilogue(z, gamma):
        hi, lo = _split2(z * z)
        ms = (_dot(hi, bd) + _dot(lo, bd)) * (1.0 / A_HEAD_DIM)
        y = z * lax.rsqrt(ms + EPS) * gamma
        rot = jnp.where(first_half,
                        -pltpu.roll(y, A_WIDTH - A_HEAD_DIM // 2, 1),
                        pltpu.roll(y, A_HEAD_DIM // 2, 1))
        return y * cos + rot * sin

    q_ref[...] = qk_epilogue(proj(_SEG_Q, A_WIDTH), qn_ref[...])
    k = qk_epilogue(proj(_SEG_K, A_WIDTH), kn_ref[...])
    k_ref[...] = k
    km_ref[0] = jnp.mean(k, axis=0, keepdims=True)
    v = proj(_SEG_V, A_WIDTH)
    v_ref[...] = v
    if kv_t_refs:
        kv_t_refs[0][0] = k.T
        kv_t_refs[1][0] = v.T
    gq_ref[...] = proj(_SEG_GQ, G_KEY_WIDTH) * (G_DK ** -0.5)
    gk_ref[...] = proj(_SEG_GK, G_KEY_WIDTH)
    gv_ref[...] = proj(_SEG_GV, G_VAL_WIDTH)
    glr = proj(_SEG_GLR, GLR_PAD)
    t = _dot(glr.astype(BF16), w2_ref[...]) + gb_ref[...]
    lg_ref[...] = (jnp.minimum(t, 0.0) - jnp.log1p(jnp.exp(-jnp.abs(t)))) * (1.0 / G_GATE_NORMALIZER)
    go_ref[...] = proj(_SEG_GOUT, G_VAL_WIDTH)
    ma_ref[...] = proj(_SEG_MA, D_MODEL)
    mb_ref[...] = proj(_SEG_MB, D_MODEL)


def _inproj(x, pos_block_of_tile, cos_tab, sin_tab, p, seq_tiles=None):
    n = x.shape[0]
    tm = TOKEN_TILE
    nt = n // tm
    row = lambda i: (i, 0)
    const = lambda i: (0, 0)
    widths = (A_WIDTH, A_WIDTH, A_WIDTH, G_KEY_WIDTH, G_KEY_WIDTH, G_VAL_WIDTH, G_KEY_WIDTH,
              G_VAL_WIDTH, D_MODEL, D_MODEL)
    out_shape = [jax.ShapeDtypeStruct((n, w), F32) for w in widths]
    out_specs = [pl.BlockSpec((tm, w), row) for w in widths]
    out_shape.append(jax.ShapeDtypeStruct((nt, 1, A_WIDTH), F32))
    out_specs.append(pl.BlockSpec((1, 1, A_WIDTH), lambda i: (i, 0, 0)))
    if seq_tiles is not None:
        for _ in range(2):
            out_shape.append(jax.ShapeDtypeStruct((nt // seq_tiles, A_WIDTH, seq_tiles * tm), F32))
            out_specs.append(pl.BlockSpec((1, A_WIDTH, tm), lambda i: (i // seq_tiles, 0, i % seq_tiles)))
    tab_spec =pl.BlockSpec((tm, A_WIDTH), lambda i: (pos_block_of_tile(i), 0))
    return pl.pallas_call(
        _inproj_kernel,
        grid=(nt,),
        in_specs=[
            pl.BlockSpec((tm, D_MODEL), row),
            pl.BlockSpec((1, D_MODEL), const),
            pl.BlockSpec((D_MODEL, _W_IN_PADDED), const),
            tab_spec, tab_spec,
            pl.BlockSpec((1, A_WIDTH), const),
            pl.BlockSpec((1, A_WIDTH), const),
            pl.BlockSpec((A_WIDTH, A_WIDTH), const),
            pl.BlockSpec((GLR_PAD, G_KEY_WIDTH), const),
            pl.BlockSpec((1, G_KEY_WIDTH), const),
        ],
        out_specs=out_specs,
        out_shape=out_shape,
        compiler_params=pltpu.CompilerParams(dimension_semantics=("arbitrary",), vmem_limit_bytes=VMEM_LIMIT),
        name="inproj",
    )(x, p["g1"], p["w_in"], cos_tab, sin_tab, p["qn"], p["kn"], p["bd"], p["w2"], p["gb"])


def _rope_tables(pos):
    half = A_HEAD_DIM // 2
    inv_freq = ROPE_THETA ** (-jnp.arange(half, dtype=F32) / half)
    ang = pos.astype(F32)[:, None] * inv_freq[None, :]
    reps = A_WIDTH // half
    return jnp.tile(jnp.cos(ang), (1, reps)), jnp.tile(jnp.sin(ang), (1, reps))


def _prep_mixer_params(norm1_g, w_in, a_q_norm, a_k_norm, g_gate_w2, g_gate_b):
    glr_off = 3 * A_WIDTH + 2 * G_KEY_WIDTH + G_VAL_WIDTH
    w_pad = jnp.concatenate([
        w_in[:, :glr_off],
        jnp.pad(w_in[:, glr_off:glr_off + G_GATE_RANK], ((0, 0), (0, GLR_PAD - G_GATE_RANK))),
        w_in[:, glr_off + G_GATE_RANK:]], axis=1).astype(BF16)
    head_id = jnp.arange(A_WIDTH) // A_HEAD_DIM
    return {
        "g1": norm1_g.reshape(1, D_MODEL),
        "w_in": w_pad,
        "qn": jnp.tile(a_q_norm, A_HEADS).reshape(1, A_WIDTH),
        "kn": jnp.tile(a_k_norm, A_HEADS).reshape(1, A_WIDTH),
        "bd": (head_id[:, None] == head_id[None, :]).astype(BF16),
        "w2": jnp.pad(g_gate_w2, ((0, GLR_PAD - G_GATE_RANK), (0, 0))).astype(BF16),
        "gb": g_gate_b.reshape(1, G_KEY_WIDTH),
    }


def _topk_rows_mask(s, valid, n_sel):
    n = s.shape[0]
    sm = jnp.where(valid, s, -jnp.inf)
    ridx = lax.broadcasted_iota(jnp.int32, s.shape, 0)
    cnt = jnp.zeros(s.shape, jnp.int32)
    for r in range(n):
        row = sm[r:r + 1, :]
        beats = (row > sm) | ((row == sm) & (r < ridx))
        cnt = cnt + beats.astype(jnp.int32)
    return valid & (cnt < n_sel)


PROMPT_BLOCKS_PER_ITER = 4


def _moba_prompt_kernel(q_ref, k_ref, v_ref, km_ref, o_ref, sel_ref):
    ob = pl.program_id(2)
    blk = MOBA_BLOCK
    nb = km_ref.shape[0]
    scale = A_HEAD_DIM ** -0.5
    kpos = lax.broadcasted_iota(jnp.int32, (blk, blk), 0)
    qpos = lax.broadcasted_iota(jnp.int32, (blk, blk), 1)
    causal = kpos <= qpos
    n_heads = LANES // A_HEAD_DIM
    head_lanes = [slice(hh * A_HEAD_DIM, (hh + 1) * A_HEAD_DIM) for hh in range(n_heads)]
    nidx = lax.broadcasted_iota(jnp.int32, (nb, blk), 0)
    qs = []
    for hh, sl in enumerate(head_lanes):
        q = q_ref[:, sl]
        s_sel = lax.dot_general(km_ref[:, sl], q, (((1,), (1,)), ((), ())),
                                precision=lax.Precision.HIGHEST, preferred_element_type=F32)
        sel_ref[hh] = _topk_rows_mask(s_sel, nidx < ob, MOBA_TOPK).astype(F32)
        qs.append((q * scale).astype(BF16))

    def scores(n, hh):
        kb = k_ref[pl.ds(n * blk, blk), head_lanes[hh]].astype(BF16)
        return _nt(kb, qs[hh])

    def values_t(n, hh):
        return v_ref[pl.ds(n * blk, blk), head_lanes[hh]].T.astype(BF16)

    init = []
    for hh in range(n_heads):
        s0 = jnp.where(causal, scores(ob, hh), -jnp.inf)
        m0 = jnp.max(s0, axis=0, keepdims=True)
        p0 = jnp.exp(s0 - m0)
        init.append((m0, jnp.sum(p0, axis=0, keepdims=True), _dot(values_t(ob, hh), p0.astype(BF16))))

    def body(it, carry):
        new = []
        for hh in range(n_heads):
            m, l, acc = carry[hh]
            blocks = [jnp.minimum(it * PROMPT_BLOCKS_PER_ITER + j, nb - 1)
                      for j in range(PROMPT_BLOCKS_PER_ITER)]
            ss = [jnp.where(sel_ref[hh, pl.ds(n, 1), :] > 0.5, scores(n, hh), -jnp.inf) for n in blocks]
            m_new = m
            for s in ss:
                m_new = jnp.maximum(m_new, jnp.max(s, axis=0, keepdims=True))
            alpha = jnp.exp(m - m_new)
            l = alpha * l
            acc = alpha * acc
            for n, s in zip(blocks, ss):
                p = jnp.exp(s - m_new)
                l = l + jnp.sum(p, axis=0, keepdims=True)
                acc = acc + _dot(values_t(n, hh), p.astype(BF16))
            new.append((m_new, l, acc))
        return tuple(new)

    n_iter = (ob + PROMPT_BLOCKS_PER_ITER - 1) // PROMPT_BLOCKS_PER_ITER
    final = lax.fori_loop(0, n_iter, body, tuple(init))
    o_ref[...] = jnp.concatenate([(acc / l).T for _, l, acc in final], axis=1)


def _moba_prompt(q, k, v, kmeans, batch, seq):
    nb = seq // MOBA_BLOCK
    hp = A_WIDTH // LANES
    return pl.pallas_call(
        _moba_prompt_kernel,
        grid=(batch, hp, nb),
        in_specs=[
            pl.BlockSpec((MOBA_BLOCK, LANES), lambda b, h, i: (b * nb + i, h)),
            pl.BlockSpec((seq, LANES), lambda b, h, i: (b, h)),
            pl.BlockSpec((seq, LANES), lambda b, h, i: (b, h)),
            pl.BlockSpec((nb, LANES), lambda b, h, i: (b, h)),
        ],
        out_specs=pl.BlockSpec((MOBA_BLOCK, LANES), lambda b, h, i: (b * nb + i, h)),
        out_shape=jax.ShapeDtypeStruct((batch * seq, A_WIDTH), F32),
        scratch_shapes=[pltpu.VMEM((LANES // A_HEAD_DIM, nb, MOBA_BLOCK), F32)],
        compiler_params=pltpu.CompilerParams(
            dimension_semantics=("arbitrary", "arbitrary", "arbitrary"), vmem_limit_bytes=VMEM_LIMIT),
        name="moba_prompt",
    )(q, k, v, kmeans)


def _cumsum_rows(tri, g):
    hi, mid, lo = _split3(g)
    return _dot(tri, hi) + _dot(tri, mid) + _dot(tri, lo)


GLA_TILE = 128


def _gla_prompt_kernel(q_ref, k_ref, v_ref, g_ref, tri_ref, o_ref, s_ref, st_ref):
    t = pl.program_id(0)
    c, sub = G_CHUNK, G_SUB
    batch = q_ref.shape[0]

    @pl.when(t == 0)
    def _():
        st_ref[...] = jnp.zeros_like(st_ref)

    tri = tri_ref[...]
    key_head = lax.broadcasted_iota(jnp.int32, (1, G_KEY_WIDTH), 1) // G_DK
    blockdiag = (lax.broadcasted_iota(jnp.int32, (G_VAL_WIDTH, G_KEY_WIDTH), 0) // G_DV
                 == lax.broadcasted_iota(jnp.int32, (G_VAL_WIDTH, G_KEY_WIDTH), 1) // G_DK)
    for ci, bi in [(ci, bi) for ci in range(GLA_TILE // c) for bi in range(batch)]:
        rows = slice(ci * c, (ci + 1) * c)
        q, k, v, g = q_ref[bi, rows, :], k_ref[bi, rows, :], v_ref[bi, rows, :], g_ref[bi, rows, :]
        b = _cumsum_rows(tri, g)
        blast = b[c - 1:c, :]
        q_inter = (q * jnp.exp(b)).astype(BF16)
        k_dec = (k * jnp.exp(blast - b)).astype(BF16)
        vb = v.astype(BF16)
        st = st_ref[bi]
        o_rows = []
        for i in range(c // sub):
            bref = b[i * sub - 1:i * sub, :] if i > 0 else jnp.zeros((1, G_KEY_WIDTH), F32)
            r = slice(i * sub, (i + 1) * sub)
            ncol = (i + 1) * sub
            q_sub = q[r, :] * jnp.exp(b[r, :] - bref)
            k_sub = (k[:ncol, :] * jnp.exp(bref - b[:ncol, :])).astype(BF16)
            q_stack = jnp.concatenate(
                [jnp.where(key_head == h, q_sub, 0.0) for h in range(G_HEADS)], axis=0).astype(BF16)
            a = _nt(q_stack, k_sub)
            rr = lax.broadcasted_iota(jnp.int32, (G_HEADS * sub, ncol), 0) % sub + i * sub
            cc = lax.broadcasted_iota(jnp.int32, (G_HEADS * sub, ncol), 1)
            av = _dot(jnp.where(cc <= rr, a, 0.0).astype(BF16), vb[:ncol, :])
            o_rows.append(jnp.concatenate(
                [av[h * sub:(h + 1) * sub, h * G_DV:(h + 1) * G_DV] for h in range(G_HEADS)], axis=1))
        o_ref[bi, rows, :] = _nt(q_inter, st.astype(BF16)) + jnp.concatenate(o_rows, axis=0)
        st_ref[bi] = st * jnp.exp(blast) + jnp.where(blockdiag, _tn(vb, k_dec), 0.0)

    @pl.when(t == pl.num_programs(0) - 1)
    def _():
        for bi in range(batch):
            for h in range(G_HEADS):
                s_ref[bi, h] = st_ref[bi, h * G_DV:(h + 1) * G_DV, h * G_DK:(h + 1) * G_DK].T


def _gla_prompt(gq, gk, gv, logg, batch, seq):
    tile = lambda w: pl.BlockSpec((batch, GLA_TILE, w), lambda t: (0, t, 0))
    tri = jnp.tril(jnp.ones((G_CHUNK, G_CHUNK), F32)).astype(BF16)
    shaped = lambda a: a.reshape(batch, seq, a.shape[-1])
    state_spec = pl.BlockSpec((batch, G_HEADS, G_DK, G_DV), lambda t: (0, 0, 0, 0))
    o, st = pl.pallas_call(
        _gla_prompt_kernel,
        grid=(seq // GLA_TILE,),
        in_specs=[tile(G_KEY_WIDTH), tile(G_KEY_WIDTH), tile(G_VAL_WIDTH), tile(G_KEY_WIDTH),
                  pl.BlockSpec((G_CHUNK, G_CHUNK), lambda t: (0, 0))],
        out_specs=[tile(G_VAL_WIDTH), state_spec],
        out_shape=[
            jax.ShapeDtypeStruct((batch, seq, G_VAL_WIDTH), F32),
            jax.ShapeDtypeStruct((batch, G_HEADS, G_DK, G_DV), F32),
        ],
        scratch_shapes=[pltpu.VMEM((batch, G_VAL_WIDTH, G_KEY_WIDTH), F32)],
        compiler_params=pltpu.CompilerParams(dimension_semantics=("arbitrary",), vmem_limit_bytes=VMEM_LIMIT),
        name="gla_prompt",
    )(shaped(gq), shaped(gk), shaped(gv), shaped(logg), tri)
    return o.reshape(batch * seq, G_VAL_WIDTH), st


GLA_SAMPLE_SEQS = 8


def _gla_sample_kernel(q_ref, k_ref, v_ref, g_ref, tri_ref, s0_ref, o_ref, s_ref, *, steps):
    q, k, v, g = q_ref[...], k_ref[...], v_ref[...], g_ref[...]
    b = _cumsum_rows(tri_ref[...], g)
    eb = jnp.exp(b)
    qd = (q * eb).astype(BF16)
    kinv = (k * jnp.exp(-b)).astype(BF16)
    vb = v.astype(BF16)
    rr = lax.broadcasted_iota(jnp.int32, (steps, steps), 0)
    cc = lax.broadcasted_iota(jnp.int32, (steps, steps), 1)
    eye = lax.broadcasted_iota(jnp.int32, (G_DK, G_DK), 0) == lax.broadcasted_iota(jnp.int32, (G_DK, G_DK), 1)
    out_rows = []
    for s in range(GLA_SAMPLE_SEQS):
        rows = slice(s * steps, (s + 1) * steps)
        blast = b[(s + 1) * steps - 1:(s + 1) * steps, :]
        k_dec = (k[rows, :] * jnp.exp(blast - b[rows, :])).astype(BF16)
        outs = []
        for h in range(G_HEADS):
            hs = slice(h * G_DK, (h + 1) * G_DK)
            vs = slice(h * G_DV, (h + 1) * G_DV)
            s0 = s0_ref[s, h]
            a = jnp.where(cc <= rr, _nt(qd[rows, hs], kinv[rows, hs]), 0.0).astype(BF16)
            outs.append(_dot(qd[rows, hs], s0.astype(BF16)) + _dot(a, vb[rows, vs]))
            dcol = jnp.sum(jnp.where(eye, jnp.exp(blast[:, hs]), 0.0), axis=1, keepdims=True)
            s_ref[s, h] = dcol * s0 + _tn(k_dec[:, hs], vb[rows, vs])
        out_rows.append(jnp.concatenate(outs, axis=1))
    o_ref[...] = jnp.concatenate(out_rows, axis=0)


def _gla_sample(gq, gk, gv, logg, state, steps):
    nseq = state.shape[0]
    sb = GLA_SAMPLE_SEQS
    rows = sb * steps
    idx = jnp.arange(rows)
    tri = ((idx[:, None] >= idx[None, :]) & (idx[:, None] // steps == idx[None, :] // steps)).astype(BF16)
    row = lambda i: (i, 0)
    return pl.pallas_call(
        functools.partial(_gla_sample_kernel, steps=steps),
        grid=(nseq // sb,),
        in_specs=[
            pl.BlockSpec((rows, G_KEY_WIDTH), row),
            pl.BlockSpec((rows, G_KEY_WIDTH), row),
            pl.BlockSpec((rows, G_VAL_WIDTH), row),
            pl.BlockSpec((rows, G_KEY_WIDTH), row),
            pl.BlockSpec((rows, rows), lambda i: (0, 0)),
            pl.BlockSpec((sb, G_HEADS, G_DK, G_DV), lambda i: (i, 0, 0, 0)),
        ],
        out_specs=[
            pl.BlockSpec((rows, G_VAL_WIDTH), row),
            pl.BlockSpec((sb, G_HEADS, G_DK, G_DV), lambda i: (i, 0, 0, 0)),
        ],
        out_shape=[
            jax.ShapeDtypeStruct((nseq * steps, G_VAL_WIDTH), F32),
            jax.ShapeDtypeStruct(state.shape, F32),
        ],
        compiler_params=pltpu.CompilerParams(dimension_semantics=("arbitrary",), vmem_limit_bytes=VMEM_LIMIT),
        name="gla_sample",
    )(gq, gk, gv, logg, tri, state)


PAGES_PER_BLOCK = MOBA_BLOCK // PAGE_SIZE


SAMPLE_PAGES_PER_STEP = 16


def _moba_sample_kernel(pt_ref, q_ref, kn_ref, vn_ref, *refs, steps, n_blocks):
    del pt_ref
    pps = SAMPLE_PAGES_PER_STEP
    k_refs, v_refs = refs[:pps], refs[pps:2 * pps]
    o_ref, wq_ref, m_ref, l_ref, ss_ref, acc_ref = refs[2 * pps:]
    g = pl.program_id(1)
    nq = A_HEADS * steps
    scale = A_HEAD_DIM ** -0.5

    @pl.when(g == 0)
    def _():
        qt = jnp.concatenate([q_ref[...]] * A_HEADS, axis=0)
        rh = lax.broadcasted_iota(jnp.int32, (nq, A_WIDTH), 0) // steps
        ch = lax.broadcasted_iota(jnp.int32, (nq, A_WIDTH), 1) // A_HEAD_DIM
        wq_ref[...] = jnp.where(rh == ch, qt * scale, 0.0).astype(BF16)

    wq = wq_ref[...]
    for bi in range(pps // PAGES_PER_BLOCK):
        n = g * (pps // PAGES_PER_BLOCK) + bi
        pages = range(bi * PAGES_PER_BLOCK, (bi + 1) * PAGES_PER_BLOCK)
        st = jnp.concatenate([_dot(wq, k_refs[j][0].astype(BF16)).T for j in pages], axis=0)
        m = jnp.max(st, axis=0, keepdims=True)
        p = jnp.exp(st - m)
        m_ref[pl.ds(n, 1), :] = m
        l_ref[pl.ds(n, 1), :] = jnp.sum(p, axis=0, keepdims=True)
        ss_ref[pl.ds(n, 1), :] = jnp.sum(st, axis=0, keepdims=True)
        pb = p.astype(BF16)
        acc = None
        for jj, j in enumerate(pages):
            part = _dot(v_refs[j][0].astype(BF16), pb[jj * PAGE_SIZE:(jj + 1) * PAGE_SIZE, :])
            acc = part if acc is None else acc + part
        acc_ref[n] = acc

    @pl.when(g == pl.num_programs(1) - 1)
    def _():
        ss = ss_ref[...]
        sel = _topk_rows_mask(ss, jnp.ones(ss.shape, jnp.bool_), min(MOBA_TOPK, n_blocks))
        s_own = _nt(kn_ref[...].astype(BF16), wq)
        tk = lax.broadcasted_iota(jnp.int32, (steps, nq), 0)
        tq = lax.broadcasted_iota(jnp.int32, (steps, nq), 1) % steps
        s_own = jnp.where(tk <= tq, s_own, -jnp.inf)
        mm = jnp.where(sel, m_ref[...], -jnp.inf)
        big = jnp.maximum(jnp.max(mm, axis=0, keepdims=True), jnp.max(s_own, axis=0, keepdims=True))
        w = jnp.where(sel, jnp.exp(mm - big), 0.0)
        p_own = jnp.exp(s_own - big)
        denom = jnp.sum(w * l_ref[...], axis=0, keepdims=True) + jnp.sum(p_own, axis=0, keepdims=True)
        acc = _tn(vn_ref[...].astype(BF16), p_own.astype(BF16))
        for j in range(n_blocks):
            acc = acc + w[j:j + 1, :] * acc_ref[j]
        out = (acc / denom).T
        o_ref[...] = jnp.concatenate(
            [out[h * steps:(h + 1) * steps, h * A_HEAD_DIM:(h + 1) * A_HEAD_DIM] for h in range(A_HEADS)], axis=1)


def _moba_sample(q, k_new, v_new, cache_kt, cache_vt, page_table, steps):
    nseq, n_pages = page_table.shape
    n_blocks = n_pages // PAGES_PER_BLOCK
    pps = SAMPLE_PAGES_PER_STEP
    nq = A_HEADS * steps
    tok = pl.BlockSpec((steps, A_WIDTH), lambda s, g, pt: (s, 0))

    def page(j):
        return pl.BlockSpec((1, A_WIDTH, PAGE_SIZE), lambda s, g, pt: (pt[s * n_pages + g * pps + j], 0, 0))

    pages = [page(j) for j in range(pps)]
    grid_spec = pltpu.PrefetchScalarGridSpec(
        num_scalar_prefetch=1,
        grid=(nseq, n_pages // pps),
        in_specs=[tok, tok, tok] + pages + pages,
        out_specs=tok,
        scratch_shapes=[
            pltpu.VMEM((nq, A_WIDTH), BF16),
            pltpu.VMEM((n_blocks, nq), F32),
            pltpu.VMEM((n_blocks, nq), F32),
            pltpu.VMEM((n_blocks, nq), F32),
            pltpu.VMEM((n_blocks, A_WIDTH, nq), F32),
        ],
    )
    return pl.pallas_call(
        functools.partial(_moba_sample_kernel, steps=steps, n_blocks=n_blocks),
        grid_spec=grid_spec,
        out_shape=jax.ShapeDtypeStruct((nseq * steps, A_WIDTH), F32),
        compiler_params=pltpu.CompilerParams(
            dimension_semantics=("arbitrary", "arbitrary"), vmem_limit_bytes=VMEM_LIMIT),
        name="moba_sample",
    )(page_table.reshape(-1), q, k_new, v_new, *([cache_kt] * pps), *([cache_vt] * pps))


def _merge_kernel(x_ref, oa_ref, ob_ref, go_ref, ma_ref, mb_ref, gn_ref, wa_ref, wb_ref, wo_ref, h_ref):
    ya = _dot(oa_ref[...].astype(BF16), wa_ref[...])
    ob = ob_ref[...]
    normed = []
    for h in range(G_HEADS):
        seg = ob[:, h * G_DV:(h + 1) * G_DV]
        normed.append(seg * lax.rsqrt(jnp.mean(seg * seg, axis=-1, keepdims=True) + EPS))
    go = go_ref[...]
    ob = jnp.concatenate(normed, axis=1) * gn_ref[...] * (go * jax.nn.sigmoid(go))
    yb = _dot(ob.astype(BF16), wb_ref[...])
    merged = jax.nn.sigmoid(ma_ref[...]) * ya + jax.nn.sigmoid(mb_ref[...]) * yb
    h_ref[...] = x_ref[...] + _dot(merged.astype(BF16), wo_ref[...])


def _merge(x, o_a, o_b, gout, ma, mb, p):
    n = x.shape[0]
    tm = TOKEN_TILE
    row = lambda i: (i, 0)
    const = lambda i: (0, 0)
    return pl.pallas_call(
        _merge_kernel,
        grid=(n // tm,),
        in_specs=[
            pl.BlockSpec((tm, D_MODEL), row),
            pl.BlockSpec((tm, A_WIDTH), row),
            pl.BlockSpec((tm, G_VAL_WIDTH), row),
            pl.BlockSpec((tm, G_VAL_WIDTH), row),
            pl.BlockSpec((tm, D_MODEL), row),
            pl.BlockSpec((tm, D_MODEL), row),
            pl.BlockSpec((1, G_VAL_WIDTH), const),
            pl.BlockSpec((A_WIDTH, D_MODEL), const),
            pl.BlockSpec((G_VAL_WIDTH, D_MODEL), const),
            pl.BlockSpec((D_MODEL, D_MODEL), const),
        ],
        out_specs=pl.BlockSpec((tm, D_MODEL), row),
        out_shape=jax.ShapeDtypeStruct((n, D_MODEL), F32),
        compiler_params=pltpu.CompilerParams(dimension_semantics=("arbitrary",), vmem_limit_bytes=VMEM_LIMIT),
        name="merge",
    )(x, o_a, o_b, gout, ma, mb, p["gn"], p["wa"], p["wb"], p["wo"])


PEER_TOKEN_TILE = 512
PEER_KEYS_PER_STEP = 8
PEER_EXPERT_TILE = PEER_KEYS_PER_STEP * P_N_KEYS
PEER_GATE_GROUP = 2
PEER_GATE_KEY_SPLIT = 1
BIG = 1e30


def _top_rows(s, count):
    cur = s
    vals = []
    for _ in range(count):
        m = jnp.max(cur, axis=0, keepdims=True)
        vals.append(m)
        cur = jnp.where(cur >= m, -jnp.inf, cur)
    return vals


def _peer_kernel(h_ref, g2_ref, wq_ref, sk1_ref, sk2_ref, u_ref, vt_ref, o_ref,
                 hb_ref, th_ref, a_ref, s2_ref, c_ref, acc_ref, act_ref, p_ref, ths_ref, as_ref):
    e = pl.program_id(1)
    tm = h_ref.shape[0]

    @pl.when(e == 0)
    def _():
        hx = h_ref[...]
        hn = hx * lax.rsqrt(jnp.mean(hx * hx, axis=-1, keepdims=True) + EPS) * g2_ref[...]
        hb = hn.astype(BF16)
        hb_ref[...] = hb
        for hd in range(P_HEADS):
            qt = _nt(wq_ref[hd * 2 * P_HALF:(hd + 1) * 2 * P_HALF, :], hb)
            s1 = _dot(sk1_ref[hd], qt[:P_HALF].astype(BF16))
            s2 = _dot(sk2_ref[hd], qt[P_HALF:].astype(BF16))
            v1 = _top_rows(s1, P_TOPK)
            v2 = _top_rows(s2, P_TOPK)
            cand = [v1[a] + v2[b] for a in range(P_TOPK) for b in range(P_TOPK // (a + 1))]
            pad = -len(cand) % 8
            cand = jnp.concatenate(cand + [jnp.full((pad, tm), -jnp.inf, F32)], axis=0)
            tops = _top_rows(cand, P_TOPK + 1)
            cut = 0.5 * (tops[P_TOPK - 1] + tops[P_TOPK])
            z = jnp.sum(jnp.where(cand >= tops[P_TOPK - 1], jnp.exp(cand - tops[0]), 0.0),
                        axis=0, keepdims=True)
            th = jnp.where(s1 >= v1[P_TOPK - 1], cut - s1, BIG)
            aa = jnp.exp(s1 - v1[0]) / z
            s2m = jnp.where(s2 >= v2[P_TOPK - 1], s2, -BIG)
            cc2 = jnp.exp(s2 - v2[0])
            for cc in range(tm // LANES):
                cols = slice(cc * LANES, (cc + 1) * LANES)
                th_ref[hd, cc] = th[:, cols]
                a_ref[hd, cc] = aa[:, cols]
                s2_ref[hd, cc] = s2m[:, cols]
                c_ref[hd, cc] = cc2[:, cols]
        acc_ref[...] = jnp.zeros_like(acc_ref)

    n_cc = tm // LANES
    group = PEER_GATE_GROUP
    half = P_N_KEYS // PEER_GATE_KEY_SPLIT

    base = pl.multiple_of(e * PEER_KEYS_PER_STEP, PEER_KEYS_PER_STEP)
    tiled = (P_HEADS, n_cc, PEER_KEYS_PER_STEP, SUBLANES, LANES)
    ths_ref[...] = jnp.broadcast_to(th_ref[:, :, pl.ds(base, PEER_KEYS_PER_STEP), :][:, :, :, None, :], tiled)
    as_ref[...] = jnp.broadcast_to(a_ref[:, :, pl.ds(base, PEER_KEYS_PER_STEP), :][:, :, :, None, :], tiled)

    def gate_unit(grp, cc, act_r, p_w):
        cols = slice(cc * LANES, (cc + 1) * LANES)
        tiles = half // SUBLANES
        for jh in range(PEER_GATE_KEY_SPLIT):
            jr = slice(jh * half, (jh + 1) * half)
            g = [jnp.zeros((tiles, SUBLANES, LANES), F32) for _ in range(group)]
            for hd in range(P_HEADS):
                s2h = s2_ref[hd, cc, jr, :].reshape(tiles, SUBLANES, LANES)
                ch = c_ref[hd, cc, jr, :].reshape(tiles, SUBLANES, LANES)
                for k in range(group):
                    i = grp * group + k
                    th = ths_ref[hd, cc, i]
                    aa = as_ref[hd, cc, i]
                    g[k] = g[k] + jnp.where(s2h >= th[None], ch, 0.0) * aa[None]
            for k in range(group):
                r0 = (grp * group + k) * P_N_KEYS + jh * half
                act = act_r[r0:r0 + half, cols]
                gl = 0.5 * act * (1.0 + lax.erf(act * (1.0 / math.sqrt(2.0))))
                p_w[r0:r0 + half, cols] = (g[k].reshape(half, LANES) * gl).astype(BF16)

    act_ref[...] = _nt(u_ref[...], hb_ref[...])
    for grp in range(PEER_KEYS_PER_STEP // group):
        for cc in range(n_cc):
            gate_unit(grp, cc, act_ref, p_ref)
    acc_ref[...] += _dot(vt_ref[0], p_ref[...])

    @pl.when(e == pl.num_programs(1) - 1)
    def _():
        o_ref[...] = h_ref[...] + acc_ref[...].T


def _peer(h, p):
    n = h.shape[0]
    tm = PEER_TOKEN_TILE
    te = PEER_EXPERT_TILE
    n_et = p["u"].shape[0] // te
    row = lambda t, e: (t, 0)
    const2 = lambda t, e: (0, 0)
    const3 = lambda t, e: (0, 0, 0)
    sel_scratch = pltpu.VMEM((P_HEADS, tm // LANES, P_N_KEYS, LANES), F32)
    return pl.pallas_call(
        _peer_kernel,
        grid=(n // tm, n_et),
        in_specs=[
            pl.BlockSpec((tm, D_MODEL), row),
            pl.BlockSpec((1, D_MODEL), const2),
            pl.BlockSpec((P_HEADS * 2 * P_HALF, D_MODEL), const2),
            pl.BlockSpec((P_HEADS, P_N_KEYS, P_HALF), const3),
            pl.BlockSpec((P_HEADS, P_N_KEYS, P_HALF), const3),
            pl.BlockSpec((te, D_MODEL), lambda t, e: (e, 0)),
            pl.BlockSpec((1, D_MODEL, te), lambda t, e: (e, 0, 0)),
        ],
        out_specs=pl.BlockSpec((tm, D_MODEL), row),
        out_shape=jax.ShapeDtypeStruct((n, D_MODEL), F32),
        scratch_shapes=[
            pltpu.VMEM((tm, D_MODEL), BF16),
            sel_scratch, sel_scratch, sel_scratch, sel_scratch,
            pltpu.VMEM((D_MODEL, tm), F32),
            pltpu.VMEM((te, tm), F32),
            pltpu.VMEM((te, tm), BF16),
            pltpu.VMEM((P_HEADS, tm // LANES, PEER_KEYS_PER_STEP, SUBLANES, LANES), F32),
            pltpu.VMEM((P_HEADS, tm // LANES, PEER_KEYS_PER_STEP, SUBLANES, LANES), F32),
        ],
        compiler_params=pltpu.CompilerParams(
            dimension_semantics=("arbitrary", "arbitrary"), vmem_limit_bytes=VMEM_LIMIT),
        name="peer",
    )(h, p["g2"], p["wq_t"], p["sk1"], p["sk2"], p["u"], p["v_t"])


def _prep_post_params(g_out_norm, w_branch_a, w_branch_b, w_out, norm2_g, peer_wq, sk1, sk2, peer_u, peer_v):
    return {
        "gn": jnp.tile(g_out_norm, G_HEADS).reshape(1, G_VAL_WIDTH),
        "wa": w_branch_a.astype(BF16),
        "wb": w_branch_b.astype(BF16),
        "wo": w_out.astype(BF16),
        "g2": norm2_g.reshape(1, D_MODEL),
        "wq_t": peer_wq.T.astype(BF16),
        "sk1": sk1.astype(BF16),
        "sk2": sk2.astype(BF16),
        "u": peer_u.astype(BF16),
        "v_t": jnp.transpose(peer_v.reshape(-1, PEER_EXPERT_TILE, D_MODEL), (0, 2, 1)).astype(BF16),
    }


def kernel(x_prompt, x_sample, cache_k, cache_v, state_gla, page_table, norm1_g, w_in, a_q_norm, a_k_norm,
           g_gate_w2, g_gate_b, g_out_norm, w_branch_a, w_branch_b, w_out, norm2_g, peer_wq,
           peer_subkeys1, peer_subkeys2, peer_u, peer_v):
    batch, seq, d = x_prompt.shape
    nseq, steps, _ = x_sample.shape
    depth, n_phys = cache_k.shape[:2]
    assert depth == 1 and d == D_MODEL
    assert seq % MOBA_BLOCK == 0 and MOBA_BLOCK == TOKEN_TILE
    past_len = page_table.shape[1] * PAGE_SIZE
    assert past_len % MOBA_BLOCK == 0 and TOKEN_TILE % steps == 0 and (nseq * steps) % TOKEN_TILE == 0
    assert math.gcd(steps, G_CHUNK) == steps and nseq % GLA_SAMPLE_SEQS == 0

    mp = _prep_mixer_params(norm1_g[0], w_in[0], a_q_norm[0], a_k_norm[0], g_gate_w2[0], g_gate_b[0])
    pp = _prep_post_params(g_out_norm[0], w_branch_a[0], w_branch_b[0], w_out[0], norm2_g[0], peer_wq[0],
                           peer_subkeys1[0], peer_subkeys2[0], peer_u[0], peer_v[0])

    nb = seq // TOKEN_TILE
    cos_p, sin_p = _rope_tables(jnp.arange(seq, dtype=jnp.int32))
    xp = x_prompt.reshape(batch * seq, d)
    q, k, v, gq, gk, gv, lg, go, ma, mb, km, k_t, v_t = _inproj(
        xp, lambda i: i % nb, cos_p, sin_p, mp, seq_tiles=nb)
    o_a = _moba_prompt(q, k, v, km.reshape(batch * nb, A_WIDTH), batch, seq)
    o_b, st_p = _gla_prompt(gq, gk, gv, lg, batch, seq)
    y_p = _peer(_merge(xp, o_a, o_b, go, ma, mb, pp), pp)

    pos_s = past_len + jnp.arange(TOKEN_TILE, dtype=jnp.int32) % steps
    cos_s, sin_s = _rope_tables(pos_s)
    xs = x_sample.reshape(nseq * steps, d)
    qs, ks, vs, gqs, gks, gvs, lgs, gos, mas, mbs, _ = _inproj(xs, lambda i: 0, cos_s, sin_s, mp)
    def pages_t(c):
        c = c.reshape(n_phys, PAGE_SIZE, A_HEADS, A_HEAD_DIM)
        return jnp.transpose(c, (0, 2, 3, 1)).reshape(n_phys, A_WIDTH, PAGE_SIZE)

    o_as = _moba_sample(qs, ks, vs, pages_t(cache_k), pages_t(cache_v), page_table, steps)
    o_bs, st_s = _gla_sample(gqs, gks, gvs, lgs, state_gla.reshape(nseq, G_HEADS, G_DK, G_DV), steps)
    y_s = _peer(_merge(xs, o_as, o_bs, gos, mas, mbs, pp), pp)

    def cache_view(a_t):
        return jnp.transpose(a_t.reshape(1, batch, A_HEADS, A_HEAD_DIM, seq), (0, 1, 4, 2, 3))

    kv_s = (1, nseq, steps, A_HEADS, A_HEAD_DIM)
    return (y_p.reshape(batch, seq, d), y_s.reshape(nseq, steps, d),
            cache_view(k_t), cache_view(v_t), st_p[None],
            ks.reshape(kv_s), vs.reshape(kv_s), st_s[None])
```

```python
import functools
import math

import jax
import jax.numpy as jnp
from jax import lax
from jax.experimental import pallas as pl
from jax.experimental.pallas import tpu as pltpu

F32 = jnp.float32
BF16 = jnp.bfloat16

D_MODEL = 1024
PAST_LEN = 8192
PAGE_SIZE = 128
A_HEADS = 8
A_HEAD_DIM = 64
A_WIDTH = A_HEADS * A_HEAD_DIM
MOBA_BLOCK = 256
MOBA_TOPK = 3
ROPE_THETA = 10000.0
G_HEADS = 4
G_DK = 64
G_DV = 128
G_KEY_WIDTH = G_HEADS * G_DK
G_VAL_WIDTH = G_HEADS * G_DV
G_GATE_RANK = 16
G_GATE_NORMALIZER = 16.0
G_CHUNK = 64
G_SUB = 16
P_HEADS = 8
P_N_KEYS = 128
P_HALF = 128
P_TOPK = 16
EPS = 1e-6

LANES = 128
VMEM_LIMIT = 56 * 1024 * 1024

TOKEN_TILE = 256
GLR_PAD = LANES

NEG = -1e30


def _nt(a, b):
    return lax.dot_general(a, b, (((1,), (1,)), ((), ())), preferred_element_type=F32)


def _tn(a, b):
    return lax.dot_general(a, b, (((0,), (0,)), ((), ())), preferred_element_type=F32)


def _dot(a, b):
    return jnp.dot(a, b, preferred_element_type=F32)


def _split2(x):
    hi = x.astype(BF16)
    lo = (x - hi.astype(F32)).astype(BF16)
    return hi, lo


def _split3(x):
    hi = x.astype(BF16)
    r = x - hi.astype(F32)
    mid = r.astype(BF16)
    lo = (r - mid.astype(F32)).astype(BF16)
    return hi, mid, lo


_SEG_Q, _SEG_K, _SEG_V = 0, 512, 1024
_SEG_GQ, _SEG_GK, _SEG_GV = 1536, 1792, 2048
_SEG_GLR = 2560
_SEG_GOUT = _SEG_GLR + GLR_PAD
_SEG_MA = _SEG_GOUT + 512
_SEG_MB = _SEG_MA + 1024
_W_IN_PADDED = _SEG_MB + 1024


def _inproj_kernel(x_ref, g1_ref, w_ref, cos_ref, sin_ref, qn_ref, kn_ref, bd_ref, w2_ref, gb_ref,
                   q_ref, k_ref, v_ref, gq_ref, gk_ref, gv_ref, lg_ref, go_ref, ma_ref, mb_ref, km_ref,
                   *kv_t_refs):
    x = x_ref[...]
    xn = x * lax.rsqrt(jnp.mean(x * x, axis=-1, keepdims=True) + EPS) * g1_ref[...]
    xb = xn.astype(BF16)

    def proj(off, width):
        return _dot(xb, w_ref[:, off:off + width])

    cos = cos_ref[...]
    sin = sin_ref[...]
    lane = lax.broadcasted_iota(jnp.int32, (1, A_WIDTH), 1)
    first_half = (lane % A_HEAD_DIM) < (A_HEAD_DIM // 2)
    bd = bd_ref[...]

    def qk_epilogue(z, gamma):
        hi, lo = _split2(z * z)
        ms = (_dot(hi, bd) + _dot(lo, bd)) * (1.0 / A_HEAD_DIM)
        y = z * lax.rsqrt(ms + EPS) * gamma
        rot = jnp.where(first_half,
                        -pltpu.roll(y, A_WIDTH - A_HEAD_DIM // 2, 1),
                        pltpu.roll(y, A_HEAD_DIM // 2, 1))
        return y * cos + rot * sin

    q_ref[...] = qk_epilogue(proj(_SEG_Q, A_WIDTH), qn_ref[...])
    k = qk_epilogue(proj(_SEG_K, A_WIDTH), kn_ref[...])
    k_ref[...] = k
    km_ref[0] = jnp.mean(k, axis=0, keepdims=True)
    v = proj(_SEG_V, A_WIDTH)
    v_ref[...] = v
    if kv_t_refs:
        kv_t_refs[0][0] = k.T
        kv_t_refs[1][0] = v.T
    gq_ref[...] = proj(_SEG_GQ, G_KEY_WIDTH) * (G_DK ** -0.5)
    gk_ref[...] = proj(_SEG_GK, G_KEY_WIDTH)
    gv_ref[...] = proj(_SEG_GV, G_VAL_WIDTH)
    glr = proj(_SEG_GLR, GLR_PAD)
    t = _dot(glr.astype(BF16), w2_ref[...]) + gb_ref[...]
    lg_ref[...] = (jnp.minimum(t, 0.0) - jnp.log1p(jnp.exp(-jnp.abs(t)))) * (1.0 / G_GATE_NORMALIZER)
    go_ref[...] = proj(_SEG_GOUT, G_VAL_WIDTH)
    ma_ref[...] = proj(_SEG_MA, D_MODEL)
    mb_ref[...] = proj(_SEG_MB, D_MODEL)


def _inproj(x, pos_block_of_tile, cos_tab, sin_tab, p, seq_tiles=None):
    n = x.shape[0]
    tm = TOKEN_TILE
    nt = n // tm
    row = lambda i: (i, 0)
    const = lambda i: (0, 0)
    widths = (A_WIDTH, A_WIDTH, A_WIDTH, G_KEY_WIDTH, G_KEY_WIDTH, G_VAL_WIDTH, G_KEY_WIDTH,
              G_VAL_WIDTH, D_MODEL, D_MODEL)
    out_shape = [jax.ShapeDtypeStruct((n, w), F32) for w in widths]
    out_specs = [pl.BlockSpec((tm, w), row) for w in widths]
    out_shape.append(jax.ShapeDtypeStruct((nt, 1, A_WIDTH), F32))
    out_specs.append(pl.BlockSpec((1, 1, A_WIDTH), lambda i: (i, 0, 0)))
    if seq_tiles is not None:
        for _ in range(2):
            out_shape.append(jax.ShapeDtypeStruct((nt // seq_tiles, A_WIDTH, seq_tiles * tm), F32))
            out_specs.append(pl.BlockSpec((1, A_WIDTH, tm), lambda i: (i // seq_tiles, 0, i % seq_tiles)))
    tab_spec =pl.BlockSpec((tm, A_WIDTH), lambda i: (pos_block_of_tile(i), 0))
    return pl.pallas_call(
        _inproj_kernel,
        grid=(nt,),
        in_specs=[
            pl.BlockSpec((tm, D_MODEL), row),
            pl.BlockSpec((1, D_MODEL), const),
            pl.BlockSpec((D_MODEL, _W_IN_PADDED), const),
            tab_spec, tab_spec,
            pl.BlockSpec((1, A_WIDTH), const),
            pl.BlockSpec((1, A_WIDTH), const),
            pl.BlockSpec((A_WIDTH, A_WIDTH), const),
            pl.BlockSpec((GLR_PAD, G_KEY_WIDTH), const),
            pl.BlockSpec((1, G_KEY_WIDTH), const),
        ],
        out_specs=out_specs,
        out_shape=out_shape,
        compiler_params=pltpu.CompilerParams(dimension_semantics=("arbitrary",), vmem_limit_bytes=VMEM_LIMIT),
        name="inproj",
    )(x, p["g1"], p["w_in"], cos_tab, sin_tab, p["qn"], p["kn"], p["bd"], p["w2"], p["gb"])


def _rope_tables(pos):
    half = A_HEAD_DIM // 2
    inv_freq = ROPE_THETA ** (-jnp.arange(half, dtype=F32) / half)
    ang = pos.astype(F32)[:, None] * inv_freq[None, :]
    reps = A_WIDTH // half
    return jnp.tile(jnp.cos(ang), (1, reps)), jnp.tile(jnp.sin(ang), (1, reps))


def _prep_mixer_params(norm1_g, w_in, a_q_norm, a_k_norm, g_gate_w2, g_gate_b):
    glr_off = 3 * A_WIDTH + 2 * G_KEY_WIDTH + G_VAL_WIDTH
    w_pad = jnp.concatenate([
        w_in[:, :glr_off],
        jnp.pad(w_in[:, glr_off:glr_off + G_GATE_RANK], ((0, 0), (0, GLR_PAD - G_GATE_RANK))),
        w_in[:, glr_off + G_GATE_RANK:]], axis=1).astype(BF16)
    head_id = jnp.arange(A_WIDTH) // A_HEAD_DIM
    return {
        "g1": norm1_g.reshape(1, D_MODEL),
        "w_in": w_pad,
        "qn": jnp.tile(a_q_norm, A_HEADS).reshape(1, A_WIDTH),
        "kn": jnp.tile(a_k_norm, A_HEADS).reshape(1, A_WIDTH),
        "bd": (head_id[:, None] == head_id[None, :]).astype(BF16),
        "w2": jnp.pad(g_gate_w2, ((0, GLR_PAD - G_GATE_RANK), (0, 0))).astype(BF16),
        "gb": g_gate_b.reshape(1, G_KEY_WIDTH),
    }


def _topk_rows_mask(s, valid, n_sel):
    n = s.shape[0]
    sm = jnp.where(valid, s, -jnp.inf)
    ridx = lax.broadcasted_iota(jnp.int32, s.shape, 0)
    cnt = jnp.zeros(s.shape, jnp.int32)
    for r in range(n):
        row = sm[r:r + 1, :]
        beats = (row > sm) | ((row == sm) & (r < ridx))
        cnt = cnt + beats.astype(jnp.int32)
    return valid & (cnt < n_sel)


PROMPT_BLOCKS_PER_ITER = 4


def _moba_prompt_kernel(q_ref, k_ref, v_ref, km_ref, o_ref, sel_ref):
    ob = pl.program_id(2)
    blk = MOBA_BLOCK
    nb = km_ref.shape[0]
    scale = A_HEAD_DIM ** -0.5
    kpos = lax.broadcasted_iota(jnp.int32, (blk, blk), 0)
    qpos = lax.broadcasted_iota(jnp.int32, (blk, blk), 1)
    causal = kpos <= qpos
    n_heads = LANES // A_HEAD_DIM
    head_lanes = [slice(hh * A_HEAD_DIM, (hh + 1) * A_HEAD_DIM) for hh in range(n_heads)]
    nidx = lax.broadcasted_iota(jnp.int32, (nb, blk), 0)
    qs = []
    for hh, sl in enumerate(head_lanes):
        q = q_ref[:, sl]
        s_sel = lax.dot_general(km_ref[:, sl], q, (((1,), (1,)), ((), ())),
                                precision=lax.Precision.HIGHEST, preferred_element_type=F32)
        sel_ref[hh] = _topk_rows_mask(s_sel, nidx < ob, MOBA_TOPK).astype(F32)
        qs.append((q * scale).astype(BF16))

    def scores(n, hh):
        kb = k_ref[pl.ds(n * blk, blk), head_lanes[hh]].astype(BF16)
        return _nt(kb, qs[hh])

    def values_t(n, hh):
        return v_ref[pl.ds(n * blk, blk), head_lanes[hh]].T.astype(BF16)

    init = []
    for hh in range(n_heads):
        s0 = jnp.where(causal, scores(ob, hh), -jnp.inf)
        m0 = jnp.max(s0, axis=0, keepdims=True)
        p0 = jnp.exp(s0 - m0)
        init.append((m0, jnp.sum(p0, axis=0, keepdims=True), _dot(values_t(ob, hh), p0.astype(BF16))))

    def body(it, carry):
        new = []
        for hh in range(n_heads):
            m, l, acc = carry[hh]
            blocks = [jnp.minimum(it * PROMPT_BLOCKS_PER_ITER + j, nb - 1)
                      for j in range(PROMPT_BLOCKS_PER_ITER)]
            ss = [jnp.where(sel_ref[hh, pl.ds(n, 1), :] > 0.5, scores(n, hh), -jnp.inf) for n in blocks]
            m_new = m
            for s in ss:
                m_new = jnp.maximum(m_new, jnp.max(s, axis=0, keepdims=True))
            alpha = jnp.exp(m - m_new)
            l = alpha * l
            acc = alpha * acc
            for n, s in zip(blocks, ss):
                p = jnp.exp(s - m_new)
                l = l + jnp.sum(p, axis=0, keepdims=True)
                acc = acc + _dot(values_t(n, hh), p.astype(BF16))
            new.append((m_new, l, acc))
        return tuple(new)

    n_iter = (ob + PROMPT_BLOCKS_PER_ITER - 1) // PROMPT_BLOCKS_PER_ITER
    final = lax.fori_loop(0, n_iter, body, tuple(init))
    o_ref[...] = jnp.concatenate([(acc / l).T for _, l, acc in final], axis=1)


def _moba_prompt(q, k, v, kmeans, batch, seq):
    nb = seq // MOBA_BLOCK
    hp = A_WIDTH // LANES
    return pl.pallas_call(
        _moba_prompt_kernel,
        grid=(batch, hp, nb),
        in_specs=[
            pl.BlockSpec((MOBA_BLOCK, LANES), lambda b, h, i: (b * nb + i, h)),
            pl.BlockSpec((seq, LANES), lambda b, h, i: (b, h)),
            pl.BlockSpec((seq, LANES), lambda b, h, i: (b, h)),
            pl.BlockSpec((nb, LANES), lambda b, h, i: (b, h)),
        ],
        out_specs=pl.BlockSpec((MOBA_BLOCK, LANES), lambda b, h, i: (b * nb + i, h)),
        out_shape=jax.ShapeDtypeStruct((batch * seq, A_WIDTH), F32),
        scratch_shapes=[pltpu.VMEM((LANES // A_HEAD_DIM, nb, MOBA_BLOCK), F32)],
        compiler_params=pltpu.CompilerParams(
            dimension_semantics=("arbitrary", "arbitrary", "arbitrary"), vmem_limit_bytes=VMEM_LIMIT),
        name="moba_prompt",
    )(q, k, v, kmeans)


def _cumsum_rows(tri, g):
    hi, mid, lo = _split3(g)
    return _dot(tri, hi) + _dot(tri, mid) + _dot(tri, lo)


GLA_TILE = 128


def _gla_prompt_kernel(q_ref, k_ref, v_ref, g_ref, tri_ref, o_ref, s_ref, st_ref):
    t = pl.program_id(0)
    c, sub = G_CHUNK, G_SUB
    batch = q_ref.shape[0]

    @pl.when(t == 0)
    def _():
        st_ref[...] = jnp.zeros_like(st_ref)

    tri = tri_ref[...]
    key_head = lax.broadcasted_iota(jnp.int32, (1, G_KEY_WIDTH), 1) // G_DK
    blockdiag = (lax.broadcasted_iota(jnp.int32, (G_VAL_WIDTH, G_KEY_WIDTH), 0) // G_DV
                 == lax.broadcasted_iota(jnp.int32, (G_VAL_WIDTH, G_KEY_WIDTH), 1) // G_DK)
    for ci, bi in [(ci, bi) for ci in range(GLA_TILE // c) for bi in range(batch)]:
        rows = slice(ci * c, (ci + 1) * c)
        q, k, v, g = q_ref[bi, rows, :], k_ref[bi, rows, :], v_ref[bi, rows, :], g_ref[bi, rows, :]
        b = _cumsum_rows(tri, g)
        blast = b[c - 1:c, :]
        q_inter = (q * jnp.exp(b)).astype(BF16)
        k_dec = (k * jnp.exp(blast - b)).astype(BF16)
        vb = v.astype(BF16)
        st = st_ref[bi]
        o_rows = []
        for i in range(c // sub):
            bref = b[i * sub - 1:i * sub, :] if i > 0 else jnp.zeros((1, G_KEY_WIDTH), F32)
            r = slice(i * sub, (i + 1) * sub)
            ncol = (i + 1) * sub
            q_sub = q[r, :] * jnp.exp(b[r, :] - bref)
            k_sub = (k[:ncol, :] * jnp.exp(bref - b[:ncol, :])).astype(BF16)
            q_stack = jnp.concatenate(
                [jnp.where(key_head == h, q_sub, 0.0) for h in range(G_HEADS)], axis=0).astype(BF16)
            a = _nt(q_stack, k_sub)
            rr = lax.broadcasted_iota(jnp.int32, (G_HEADS * sub, ncol), 0) % sub + i * sub
            cc = lax.broadcasted_iota(jnp.int32, (G_HEADS * sub, ncol), 1)
            av = _dot(jnp.where(cc <= rr, a, 0.0).astype(BF16), vb[:ncol, :])
            o_rows.append(jnp.concatenate(
                [av[h * sub:(h + 1) * sub, h * G_DV:(h + 1) * G_DV] for h in range(G_HEADS)], axis=1))
        o_ref[bi, rows, :] = _nt(q_inter, st.astype(BF16)) + jnp.concatenate(o_rows, axis=0)
        st_ref[bi] = st * jnp.exp(blast) + jnp.where(blockdiag, _tn(vb, k_dec), 0.0)

    @pl.when(t == pl.num_programs(0) - 1)
    def _():
        for bi in range(batch):
            for h in range(G_HEADS):
                s_ref[bi, h] = st_ref[bi, h * G_DV:(h + 1) * G_DV, h * G_DK:(h + 1) * G_DK].T


def _gla_prompt(gq, gk, gv, logg, batch, seq):
    tile = lambda w: pl.BlockSpec((batch, GLA_TILE, w), lambda t: (0, t, 0))
    tri = jnp.tril(jnp.ones((G_CHUNK, G_CHUNK), F32)).astype(BF16)
    shaped = lambda a: a.reshape(batch, seq, a.shape[-1])
    state_spec = pl.BlockSpec((batch, G_HEADS, G_DK, G_DV), lambda t: (0, 0, 0, 0))
    o, st = pl.pallas_call(
        _gla_prompt_kernel,
        grid=(seq // GLA_TILE,),
        in_specs=[tile(G_KEY_WIDTH), tile(G_KEY_WIDTH), tile(G_VAL_WIDTH), tile(G_KEY_WIDTH),
                  pl.BlockSpec((G_CHUNK, G_CHUNK), lambda t: (0, 0))],
        out_specs=[tile(G_VAL_WIDTH), state_spec],
        out_shape=[
            jax.ShapeDtypeStruct((batch, seq, G_VAL_WIDTH), F32),
            jax.ShapeDtypeStruct((batch, G_HEADS, G_DK, G_DV), F32),
        ],
        scratch_shapes=[pltpu.VMEM((batch, G_VAL_WIDTH, G_KEY_WIDTH), F32)],
        compiler_params=pltpu.CompilerParams(dimension_semantics=("arbitrary",), vmem_limit_bytes=VMEM_LIMIT),
        name="gla_prompt",
    )(shaped(gq), shaped(gk), shaped(gv), shaped(logg), tri)
    return o.reshape(batch * seq, G_VAL_WIDTH), st


GLA_SAMPLE_SEQS = 8


def _gla_sample_kernel(q_ref, k_ref, v_ref, g_ref, tri_ref, s0_ref, o_ref, s_ref, *, steps):
    q, k, v, g = q_ref[...], k_ref[...], v_ref[...], g_ref[...]
    b = _cumsum_rows(tri_ref[...], g)
    eb = jnp.exp(b)
    qd = (q * eb).astype(BF16)
    kinv = (k * jnp.exp(-b)).astype(BF16)
    vb = v.astype(BF16)
    rr = lax.broadcasted_iota(jnp.int32, (steps, steps), 0)
    cc = lax.broadcasted_iota(jnp.int32, (steps, steps), 1)
    eye = lax.broadcasted_iota(jnp.int32, (G_DK, G_DK), 0) == lax.broadcasted_iota(jnp.int32, (G_DK, G_DK), 1)
    out_rows = []
    for s in range(GLA_SAMPLE_SEQS):
        rows = slice(s * steps, (s + 1) * steps)
        blast = b[(s + 1) * steps - 1:(s + 1) * steps, :]
        k_dec = (k[rows, :] * jnp.exp(blast - b[rows, :])).astype(BF16)
        outs = []
        for h in range(G_HEADS):
            hs = slice(h * G_DK, (h + 1) * G_DK)
            vs = slice(h * G_DV, (h + 1) * G_DV)
            s0 = s0_ref[s, h]
            a = jnp.where(cc <= rr, _nt(qd[rows, hs], kinv[rows, hs]), 0.0).astype(BF16)
            outs.append(_dot(qd[rows, hs], s0.astype(BF16)) + _dot(a, vb[rows, vs]))
            dcol = jnp.sum(jnp.where(eye, jnp.exp(blast[:, hs]), 0.0), axis=1, keepdims=True)
            s_ref[s, h] = dcol * s0 + _tn(k_dec[:, hs], vb[rows, vs])
        out_rows.append(jnp.concatenate(outs, axis=1))
    o_ref[...] = jnp.concatenate(out_rows, axis=0)


def _gla_sample(gq, gk, gv, logg, state, steps):
    nseq = state.shape[0]
    sb = GLA_SAMPLE_SEQS
    rows = sb * steps
    idx = jnp.arange(rows)
    tri = ((idx[:, None] >= idx[None, :]) & (idx[:, None] // steps == idx[None, :] // steps)).astype(BF16)
    row = lambda i: (i, 0)
    return pl.pallas_call(
        functools.partial(_gla_sample_kernel, steps=steps),
        grid=(nseq // sb,),
        in_specs=[
            pl.BlockSpec((rows, G_KEY_WIDTH), row),
            pl.BlockSpec((rows, G_KEY_WIDTH), row),
            pl.BlockSpec((rows, G_VAL_WIDTH), row),
            pl.BlockSpec((rows, G_KEY_WIDTH), row),
            pl.BlockSpec((rows, rows), lambda i: (0, 0)),
            pl.BlockSpec((sb, G_HEADS, G_DK, G_DV), lambda i: (i, 0, 0, 0)),
        ],
        out_specs=[
            pl.BlockSpec((rows, G_VAL_WIDTH), row),
            pl.BlockSpec((sb, G_HEADS, G_DK, G_DV), lambda i: (i, 0, 0, 0)),
        ],
        out_shape=[
            jax.ShapeDtypeStruct((nseq * steps, G_VAL_WIDTH), F32),
            jax.ShapeDtypeStruct(state.shape, F32),
        ],
        compiler_params=pltpu.CompilerParams(dimension_semantics=("arbitrary",), vmem_limit_bytes=VMEM_LIMIT),
        name="gla_sample",
    )(gq, gk, gv, logg, tri, state)


PAGES_PER_BLOCK = MOBA_BLOCK // PAGE_SIZE


SAMPLE_PAGES_PER_STEP = 16


def _moba_sample_kernel(pt_ref, q_ref, kn_ref, vn_ref, *refs, steps, n_blocks):
    del pt_ref
    pps = SAMPLE_PAGES_PER_STEP
    k_refs, v_refs = refs[:pps], refs[pps:2 * pps]
    o_ref, wq_ref, m_ref, l_ref, ss_ref, acc_ref = refs[2 * pps:]
    g = pl.program_id(1)
    nq = A_HEADS * steps
    scale = A_HEAD_DIM ** -0.5

    @pl.when(g == 0)
    def _():
        qt = jnp.concatenate([q_ref[...]] * A_HEADS, axis=0)
        rh = lax.broadcasted_iota(jnp.int32, (nq, A_WIDTH), 0) // steps
        ch = lax.broadcasted_iota(jnp.int32, (nq, A_WIDTH), 1) // A_HEAD_DIM
        wq_ref[...] = jnp.where(rh == ch, qt * scale, 0.0).astype(BF16)

    wq = wq_ref[...]
    for bi in range(pps // PAGES_PER_BLOCK):
        n = g * (pps // PAGES_PER_BLOCK) + bi
        pages = range(bi * PAGES_PER_BLOCK, (bi + 1) * PAGES_PER_BLOCK)
        st = jnp.concatenate([_dot(wq, k_refs[j][0].astype(BF16)).T for j in pages], axis=0)
        m = jnp.max(st, axis=0, keepdims=True)
        p = jnp.exp(st - m)
        m_ref[pl.ds(n, 1), :] = m
        l_ref[pl.ds(n, 1), :] = jnp.sum(p, axis=0, keepdims=True)
        ss_ref[pl.ds(n, 1), :] = jnp.sum(st, axis=0, keepdims=True)
        pb = p.astype(BF16)
        acc = None
        for jj, j in enumerate(pages):
            part = _dot(v_refs[j][0].astype(BF16), pb[jj * PAGE_SIZE:(jj + 1) * PAGE_SIZE, :])
            acc = part if acc is None else acc + part
        acc_ref[n] = acc

    @pl.when(g == pl.num_programs(1) - 1)
    def _():
        ss = ss_ref[...]
        sel = _topk_rows_mask(ss, jnp.ones(ss.shape, jnp.bool_), min(MOBA_TOPK, n_blocks))
        s_own = _nt(kn_ref[...].astype(BF16), wq)
        tk = lax.broadcasted_iota(jnp.int32, (steps, nq), 0)
        tq = lax.broadcasted_iota(jnp.int32, (steps, nq), 1) % steps
        s_own = jnp.where(tk <= tq, s_own, -jnp.inf)
        mm = jnp.where(sel, m_ref[...], -jnp.inf)
        big = jnp.maximum(jnp.max(mm, axis=0, keepdims=True), jnp.max(s_own, axis=0, keepdims=True))
        w = jnp.where(sel, jnp.exp(mm - big), 0.0)
        p_own = jnp.exp(s_own - big)
        denom = jnp.sum(w * l_ref[...], axis=0, keepdims=True) + jnp.sum(p_own, axis=0, keepdims=True)
        acc = _tn(vn_ref[...].astype(BF16), p_own.astype(BF16))
        for j in range(n_blocks):
            acc = acc + w[j:j + 1, :] * acc_ref[j]
        out = (acc / denom).T
        o_ref[...] = jnp.concatenate(
            [out[h * steps:(h + 1) * steps, h * A_HEAD_DIM:(h + 1) * A_HEAD_DIM] for h in range(A_HEADS)], axis=1)


def _moba_sample(q, k_new, v_new, cache_kt, cache_vt, page_table, steps):
    nseq, n_pages = page_table.shape
    n_blocks = n_pages // PAGES_PER_BLOCK
    pps = SAMPLE_PAGES_PER_STEP
    nq = A_HEADS * steps
    tok = pl.BlockSpec((steps, A_WIDTH), lambda s, g, pt: (s, 0))

    def page(j):
        return pl.BlockSpec((1, A_WIDTH, PAGE_SIZE), lambda s, g, pt: (pt[s * n_pages + g * pps + j], 0, 0))

    pages = [page(j) for j in range(pps)]
    grid_spec = pltpu.PrefetchScalarGridSpec(
        num_scalar_prefetch=1,
        grid=(nseq, n_pages // pps),
        in_specs=[tok, tok, tok] + pages + pages,
        out_specs=tok,
        scratch_shapes=[
            pltpu.VMEM((nq, A_WIDTH), BF16),
            pltpu.VMEM((n_blocks, nq), F32),
            pltpu.VMEM((n_blocks, nq), F32),
            pltpu.VMEM((n_blocks, nq), F32),
            pltpu.VMEM((n_blocks, A_WIDTH, nq), F32),
        ],
    )
    return pl.pallas_call(
        functools.partial(_moba_sample_kernel, steps=steps, n_blocks=n_blocks),
        grid_spec=grid_spec,
        out_shape=jax.ShapeDtypeStruct((nseq * steps, A_WIDTH), F32),
        compiler_params=pltpu.CompilerParams(
            dimension_semantics=("arbitrary", "arbitrary"), vmem_limit_bytes=VMEM_LIMIT),
        name="moba_sample",
    )(page_table.reshape(-1), q, k_new, v_new, *([cache_kt] * pps), *([cache_vt] * pps))


def _merge_kernel(x_ref, oa_ref, ob_ref, go_ref, ma_ref, mb_ref, gn_ref, wa_ref, wb_ref, wo_ref, h_ref):
    ya = _dot(oa_ref[...].astype(BF16), wa_ref[...])
    ob = ob_ref[...]
    normed = []
    for h in range(G_HEADS):
        seg = ob[:, h * G_DV:(h + 1) * G_DV]
        normed.append(seg * lax.rsqrt(jnp.mean(seg * seg, axis=-1, keepdims=True) + EPS))
    go = go_ref[...]
    ob = jnp.concatenate(normed, axis=1) * gn_ref[...] * (go * jax.nn.sigmoid(go))
    yb = _dot(ob.astype(BF16), wb_ref[...])
    merged = jax.nn.sigmoid(ma_ref[...]) * ya + jax.nn.sigmoid(mb_ref[...]) * yb
    h_ref[...] = x_ref[...] + _dot(merged.astype(BF16), wo_ref[...])


def _merge(x, o_a, o_b, gout, ma, mb, p):
    n = x.shape[0]
    tm = TOKEN_TILE
    row = lambda i: (i, 0)
    const = lambda i: (0, 0)
    return pl.pallas_call(
        _merge_kernel,
        grid=(n // tm,),
        in_specs=[
            pl.BlockSpec((tm, D_MODEL), row),
            pl.BlockSpec((tm, A_WIDTH), row),
            pl.BlockSpec((tm, G_VAL_WIDTH), row),
            pl.BlockSpec((tm, G_VAL_WIDTH), row),
            pl.BlockSpec((tm, D_MODEL), row),
            pl.BlockSpec((tm, D_MODEL), row),
            pl.BlockSpec((1, G_VAL_WIDTH), const),
            pl.BlockSpec((A_WIDTH, D_MODEL), const),
            pl.BlockSpec((G_VAL_WIDTH, D_MODEL), const),
            pl.BlockSpec((D_MODEL, D_MODEL), const),
        ],
        out_specs=pl.BlockSpec((tm, D_MODEL), row),
        out_shape=jax.ShapeDtypeStruct((n, D_MODEL), F32),
        compiler_params=pltpu.CompilerParams(dimension_semantics=("arbitrary",), vmem_limit_bytes=VMEM_LIMIT),
        name="merge",
    )(x, o_a, o_b, gout, ma, mb, p["gn"], p["wa"], p["wb"], p["wo"])


PEER_TOKEN_TILE = 512
PEER_KEYS_PER_STEP = 8
PEER_EXPERT_TILE = PEER_KEYS_PER_STEP * P_N_KEYS
PEER_GATE_GROUP = 4
PEER_GATE_KEY_SPLIT = 2
BIG = 1e30


def _top_rows(s, count):
    cur = s
    vals = []
    for _ in range(count):
        m = jnp.max(cur, axis=0, keepdims=True)
        vals.append(m)
        cur = jnp.where(cur >= m, -jnp.inf, cur)
    return vals


def _peer_kernel(h_ref, g2_ref, wq_ref, sk1_ref, sk2_ref, u_ref, vt_ref, o_ref,
                 hb_ref, th_ref, a_ref, s2_ref, c_ref, acc_ref, act_ref, p_ref, ths_ref, as_ref):
    e = pl.program_id(1)
    tm = h_ref.shape[0]

    @pl.when(e == 0)
    def _():
        hx = h_ref[...]
        hn = hx * lax.rsqrt(jnp.mean(hx * hx, axis=-1, keepdims=True) + EPS) * g2_ref[...]
        hb = hn.astype(BF16)
        hb_ref[...] = hb
        for hd in range(P_HEADS):
            qt = _nt(wq_ref[hd * 2 * P_HALF:(hd + 1) * 2 * P_HALF, :], hb)
            s1 = _dot(sk1_ref[hd], qt[:P_HALF].astype(BF16))
            s2 = _dot(sk2_ref[hd], qt[P_HALF:].astype(BF16))
            v1 = _top_rows(s1, P_TOPK)
            v2 = _top_rows(s2, P_TOPK)
            cand = [v1[a] + v2[b] for a in range(P_TOPK) for b in range(P_TOPK // (a + 1))]
            pad = -len(cand) % 8
            cand = jnp.concatenate(cand + [jnp.full((pad, tm), -jnp.inf, F32)], axis=0)
            tops = _top_rows(cand, P_TOPK + 1)
            cut = 0.5 * (tops[P_TOPK - 1] + tops[P_TOPK])
            z = jnp.sum(jnp.where(cand >= tops[P_TOPK - 1], jnp.exp(cand - tops[0]), 0.0),
                        axis=0, keepdims=True)
            th = jnp.where(s1 >= v1[P_TOPK - 1], cut - s1, BIG)
            aa = jnp.exp(s1 - v1[0]) / z
            s2m = jnp.where(s2 >= v2[P_TOPK - 1], s2, -BIG)
            cc2 = jnp.exp(s2 - v2[0])
            for cc in range(tm // LANES):
                cols = slice(cc * LANES, (cc + 1) * LANES)
                th_ref[hd, cc] = th[:, cols]
                a_ref[hd, cc] = aa[:, cols]
                s2_ref[hd, cc] = s2m[:, cols]
                c_ref[hd, cc, :P_N_KEYS, :] = cc2[:, cols]
        acc_ref[...] = jnp.zeros_like(acc_ref)

    n_cc = tm // LANES
    group = PEER_GATE_GROUP
    half = P_N_KEYS // PEER_GATE_KEY_SPLIT

    base = pl.multiple_of(e * PEER_KEYS_PER_STEP, PEER_KEYS_PER_STEP)
    ths_ref[...] = th_ref[:, :, pl.ds(base, PEER_KEYS_PER_STEP), :]
    as_ref[...] = a_ref[:, :, pl.ds(base, PEER_KEYS_PER_STEP), :]

    def gate_unit(grp, cc, act_r, p_w):
        cols = slice(cc * LANES, (cc + 1) * LANES)
        for jh in range(PEER_GATE_KEY_SPLIT):
            jr = slice(jh * half, (jh + 1) * half)
            g = [jnp.zeros((half, LANES), F32) for _ in range(group)]
            for hd in range(P_HEADS):
                s2h = s2_ref[hd, cc, jr, :]
                ch = c_ref[hd, cc, jr, :]
                for k in range(group):
                    i = grp * group + k
                    th = ths_ref[hd, cc, i:i + 1, :]
                    aa = as_ref[hd, cc, i:i + 1, :]
                    g[k] = g[k] + jnp.where(s2h >= th, ch, 0.0) * aa
            for k in range(group):
                r0 = (grp * group + k) * P_N_KEYS + jh * half
                act = act_r[r0:r0 + half, cols]
                gl = 0.5 * act * (1.0 + lax.erf(act * (1.0 / math.sqrt(2.0))))
                p_w[r0:r0 + half, cols] = (g[k] * gl).astype(BF16)

    act_ref[...] = _nt(u_ref[...], hb_ref[...])
    for grp in range(PEER_KEYS_PER_STEP // group):
        for cc in range(n_cc):
            gate_unit(grp, cc, act_ref, p_ref)
    acc_ref[...] += _dot(vt_ref[0], p_ref[...])

    @pl.when(e == pl.num_programs(1) - 1)
    def _():
        o_ref[...] = h_ref[...] + acc_ref[...].T


def _peer(h, p):
    n = h.shape[0]
    tm = PEER_TOKEN_TILE
    te = PEER_EXPERT_TILE
    n_et = p["u"].shape[0] // te
    row = lambda t, e: (t, 0)
    const2 = lambda t, e: (0, 0)
    const3 = lambda t, e: (0, 0, 0)
    sel_scratch = pltpu.VMEM((P_HEADS, tm // LANES, P_N_KEYS, LANES), F32)
    return pl.pallas_call(
        _peer_kernel,
        grid=(n // tm, n_et),
        in_specs=[
            pl.BlockSpec((tm, D_MODEL), row),
            pl.BlockSpec((1, D_MODEL), const2),
            pl.BlockSpec((P_HEADS * 2 * P_HALF, D_MODEL), const2),
            pl.BlockSpec((P_HEADS, P_N_KEYS, P_HALF), const3),
            pl.BlockSpec((P_HEADS, P_N_KEYS, P_HALF), const3),
            pl.BlockSpec((te, D_MODEL), lambda t, e: (e, 0)),
            pl.BlockSpec((1, D_MODEL, te), lambda t, e: (e, 0, 0)),
        ],
        out_specs=pl.BlockSpec((tm, D_MODEL), row),
        out_shape=jax.ShapeDtypeStruct((n, D_MODEL), F32),
        scratch_shapes=[
            pltpu.VMEM((tm, D_MODEL), BF16),
            sel_scratch, sel_scratch, sel_scratch,
            pltpu.VMEM((P_HEADS, tm // LANES, P_N_KEYS + 8, LANES), F32),
            pltpu.VMEM((D_MODEL, tm), F32),
            pltpu.VMEM((te, tm), F32),
            pltpu.VMEM((te, tm), BF16),
            pltpu.VMEM((P_HEADS, tm // LANES, PEER_KEYS_PER_STEP, LANES), F32),
            pltpu.VMEM((P_HEADS, tm // LANES, PEER_KEYS_PER_STEP, LANES), F32),
        ],
        compiler_params=pltpu.CompilerParams(
            dimension_semantics=("arbitrary", "arbitrary"), vmem_limit_bytes=VMEM_LIMIT),
        name="peer",
    )(h, p["g2"], p["wq_t"], p["sk1"], p["sk2"], p["u"], p["v_t"])


def _prep_post_params(g_out_norm, w_branch_a, w_branch_b, w_out, norm2_g, peer_wq, sk1, sk2, peer_u, peer_v):
    return {
        "gn": jnp.tile(g_out_norm, G_HEADS).reshape(1, G_VAL_WIDTH),
        "wa": w_branch_a.astype(BF16),
        "wb": w_branch_b.astype(BF16),
        "wo": w_out.astype(BF16),
        "g2": norm2_g.reshape(1, D_MODEL),
        "wq_t": peer_wq.T.astype(BF16),
        "sk1": sk1.astype(BF16),
        "sk2": sk2.astype(BF16),
        "u": peer_u.astype(BF16),
        "v_t": jnp.transpose(peer_v.reshape(-1, PEER_EXPERT_TILE, D_MODEL), (0, 2, 1)).astype(BF16),
    }


def kernel(x_prompt, x_sample, cache_k, cache_v, state_gla, page_table, norm1_g, w_in, a_q_norm, a_k_norm,
           g_gate_w2, g_gate_b, g_out_norm, w_branch_a, w_branch_b, w_out, norm2_g, peer_wq,
           peer_subkeys1, peer_subkeys2, peer_u, peer_v):
    batch, seq, d = x_prompt.shape
    nseq, steps, _ = x_sample.shape
    depth, n_phys = cache_k.shape[:2]
    assert depth == 1 and d == D_MODEL
    assert seq % MOBA_BLOCK == 0 and MOBA_BLOCK == TOKEN_TILE
    past_len = page_table.shape[1] * PAGE_SIZE
    assert past_len % MOBA_BLOCK == 0 and TOKEN_TILE % steps == 0 and (nseq * steps) % TOKEN_TILE == 0
    assert math.gcd(steps, G_CHUNK) == steps and nseq % GLA_SAMPLE_SEQS == 0

    mp = _prep_mixer_params(norm1_g[0], w_in[0], a_q_norm[0], a_k_norm[0], g_gate_w2[0], g_gate_b[0])
    pp = _prep_post_params(g_out_norm[0], w_branch_a[0], w_branch_b[0], w_out[0], norm2_g[0], peer_wq[0],
                           peer_subkeys1[0], peer_subkeys2[0], peer_u[0], peer_v[0])

    nb = seq // TOKEN_TILE
    cos_p, sin_p = _rope_tables(jnp.arange(seq, dtype=jnp.int32))
    xp = x_prompt.reshape(batch * seq, d)
    q, k, v, gq, gk, gv, lg, go, ma, mb, km, k_t, v_t = _inproj(
        xp, lambda i: i % nb, cos_p, sin_p, mp, seq_tiles=nb)
    o_a = _moba_prompt(q, k, v, km.reshape(batch * nb, A_WIDTH), batch, seq)
    o_b, st_p = _gla_prompt(gq, gk, gv, lg, batch, seq)
    y_p = _peer(_merge(xp, o_a, o_b, go, ma, mb, pp), pp)

    pos_s = past_len + jnp.arange(TOKEN_TILE, dtype=jnp.int32) % steps
    cos_s, sin_s = _rope_tables(pos_s)
    xs = x_sample.reshape(nseq * steps, d)
    qs, ks, vs, gqs, gks, gvs, lgs, gos, mas, mbs, _ = _inproj(xs, lambda i: 0, cos_s, sin_s, mp)
    def pages_t(c):
        c = c.reshape(n_phys, PAGE_SIZE, A_HEADS, A_HEAD_DIM)
        return jnp.transpose(c, (0, 2, 3, 1)).reshape(n_phys, A_WIDTH, PAGE_SIZE)

    o_as = _moba_sample(qs, ks, vs, pages_t(cache_k), pages_t(cache_v), page_table, steps)
    o_bs, st_s = _gla_sample(gqs, gks, gvs, lgs, state_gla.reshape(nseq, G_HEADS, G_DK, G_DV), steps)
    y_s = _peer(_merge(xs, o_as, o_bs, gos, mas, mbs, pp), pp)

    def cache_view(a_t):
        return jnp.transpose(a_t.reshape(1, batch, A_HEADS, A_HEAD_DIM, seq), (0, 1, 4, 2, 3))

    kv_s = (1, nseq, steps, A_HEADS, A_HEAD_DIM)
    return (y_p.reshape(batch, seq, d), y_s.reshape(nseq, steps, d),
            cache_view(k_t), cache_view(v_t), st_p[None],
            ks.reshape(kv_s), vs.reshape(kv_s), st_s[None])
```

```python
import functools
import math

import jax
import jax.numpy as jnp
from jax import lax
from jax.experimental import pallas as pl
from jax.experimental.pallas import tpu as pltpu

F32 = jnp.float32
BF16 = jnp.bfloat16

D_MODEL = 1024
PAST_LEN = 8192
PAGE_SIZE = 128
A_HEADS = 8
A_HEAD_DIM = 64
A_WIDTH = A_HEADS * A_HEAD_DIM
MOBA_BLOCK = 256
MOBA_TOPK = 3
ROPE_THETA = 10000.0
G_HEADS = 4
G_DK = 64
G_DV = 128
G_KEY_WIDTH = G_HEADS * G_DK
G_VAL_WIDTH = G_HEADS * G_DV
G_GATE_RANK = 16
G_GATE_NORMALIZER = 16.0
G_CHUNK = 64
G_SUB = 16
P_HEADS = 8
P_N_KEYS = 128
P_HALF = 128
P_TOPK = 16
EPS = 1e-6

LANES = 128
VMEM_LIMIT = 56 * 1024 * 1024

TOKEN_TILE = 256
GLR_PAD = LANES

NEG = -1e30


def _nt(a, b):
    return lax.dot_general(a, b, (((1,), (1,)), ((), ())), preferred_element_type=F32)


def _tn(a, b):
    return lax.dot_general(a, b, (((0,), (0,)), ((), ())), preferred_element_type=F32)


def _dot(a, b):
    return jnp.dot(a, b, preferred_element_type=F32)


def _split2(x):
    hi = x.astype(BF16)
    lo = (x - hi.astype(F32)).astype(BF16)
    return hi, lo


def _split3(x):
    hi = x.astype(BF16)
    r = x - hi.astype(F32)
    mid = r.astype(BF16)
    lo = (r - mid.astype(F32)).astype(BF16)
    return hi, mid, lo


_SEG_Q, _SEG_K, _SEG_V = 0, 512, 1024
_SEG_GQ, _SEG_GK, _SEG_GV = 1536, 1792, 2048
_SEG_GLR = 2560
_SEG_GOUT = _SEG_GLR + GLR_PAD
_SEG_MA = _SEG_GOUT + 512
_SEG_MB = _SEG_MA + 1024
_W_IN_PADDED = _SEG_MB + 1024


def _inproj_kernel(x_ref, g1_ref, w_ref, cos_ref, sin_ref, qn_ref, kn_ref, bd_ref, w2_ref, gb_ref,
                   q_ref, k_ref, v_ref, gq_ref, gk_ref, gv_ref, lg_ref, go_ref, ma_ref, mb_ref, km_ref,
                   *kv_t_refs):
    x = x_ref[...]
    xn = x * lax.rsqrt(jnp.mean(x * x, axis=-1, keepdims=True) + EPS) * g1_ref[...]
    xb = xn.astype(BF16)

    def proj(off, width):
        return _dot(xb, w_ref[:, off:off + width])

    cos = cos_ref[...]
    sin = sin_ref[...]
    lane = lax.broadcasted_iota(jnp.int32, (1, A_WIDTH), 1)
    first_half = (lane % A_HEAD_DIM) < (A_HEAD_DIM // 2)
    bd = bd_ref[...]

    def qk_epilogue(z, gamma):
        hi, lo = _split2(z * z)
        ms = (_dot(hi, bd) + _dot(lo, bd)) * (1.0 / A_HEAD_DIM)
        y = z * lax.rsqrt(ms + EPS) * gamma
        rot = jnp.where(first_half,
                        -pltpu.roll(y, A_WIDTH - A_HEAD_DIM // 2, 1),
                        pltpu.roll(y, A_HEAD_DIM // 2, 1))
        return y * cos + rot * sin

    q_ref[...] = qk_epilogue(proj(_SEG_Q, A_WIDTH), qn_ref[...])
    k = qk_epilogue(proj(_SEG_K, A_WIDTH), kn_ref[...])
    k_ref[...] = k
    km_ref[0] = jnp.mean(k, axis=0, keepdims=True)
    v = proj(_SEG_V, A_WIDTH)
    v_ref[...] = v
    if kv_t_refs:
        kv_t_refs[0][0] = k.T
        kv_t_refs[1][0] = v.T
    gq_ref[...] = proj(_SEG_GQ, G_KEY_WIDTH) * (G_DK ** -0.5)
    gk_ref[...] = proj(_SEG_GK, G_KEY_WIDTH)
    gv_ref[...] = proj(_SEG_GV, G_VAL_WIDTH)
    glr = proj(_SEG_GLR, GLR_PAD)
    t = _dot(glr.astype(BF16), w2_ref[...]) + gb_ref[...]
    lg_ref[...] = (jnp.minimum(t, 0.0) - jnp.log1p(jnp.exp(-jnp.abs(t)))) * (1.0 / G_GATE_NORMALIZER)
    go_ref[...] = proj(_SEG_GOUT, G_VAL_WIDTH)
    ma_ref[...] = proj(_SEG_MA, D_MODEL)
    mb_ref[...] = proj(_SEG_MB, D_MODEL)


def _inproj(x, pos_block_of_tile, cos_tab, sin_tab, p, seq_tiles=None):
    n = x.shape[0]
    tm = TOKEN_TILE
    nt = n // tm
    row = lambda i: (i, 0)
    const = lambda i: (0, 0)
    widths = (A_WIDTH, A_WIDTH, A_WIDTH, G_KEY_WIDTH, G_KEY_WIDTH, G_VAL_WIDTH, G_KEY_WIDTH,
              G_VAL_WIDTH, D_MODEL, D_MODEL)
    out_shape = [jax.ShapeDtypeStruct((n, w), F32) for w in widths]
    out_specs = [pl.BlockSpec((tm, w), row) for w in widths]
    out_shape.append(jax.ShapeDtypeStruct((nt, 1, A_WIDTH), F32))
    out_specs.append(pl.BlockSpec((1, 1, A_WIDTH), lambda i: (i, 0, 0)))
    if seq_tiles is not None:
        for _ in range(2):
            out_shape.append(jax.ShapeDtypeStruct((nt // seq_tiles, A_WIDTH, seq_tiles * tm), F32))
            out_specs.append(pl.BlockSpec((1, A_WIDTH, tm), lambda i: (i // seq_tiles, 0, i % seq_tiles)))
    tab_spec =pl.BlockSpec((tm, A_WIDTH), lambda i: (pos_block_of_tile(i), 0))
    return pl.pallas_call(
        _inproj_kernel,
        grid=(nt,),
        in_specs=[
            pl.BlockSpec((tm, D_MODEL), row),
            pl.BlockSpec((1, D_MODEL), const),
            pl.BlockSpec((D_MODEL, _W_IN_PADDED), const),
            tab_spec, tab_spec,
            pl.BlockSpec((1, A_WIDTH), const),
            pl.BlockSpec((1, A_WIDTH), const),
            pl.BlockSpec((A_WIDTH, A_WIDTH), const),
            pl.BlockSpec((GLR_PAD, G_KEY_WIDTH), const),
            pl.BlockSpec((1, G_KEY_WIDTH), const),
        ],
        out_specs=out_specs,
        out_shape=out_shape,
        compiler_params=pltpu.CompilerParams(dimension_semantics=("arbitrary",), vmem_limit_bytes=VMEM_LIMIT),
        name="inproj",
    )(x, p["g1"], p["w_in"], cos_tab, sin_tab, p["qn"], p["kn"], p["bd"], p["w2"], p["gb"])


def _rope_tables(pos):
    half = A_HEAD_DIM // 2
    inv_freq = ROPE_THETA ** (-jnp.arange(half, dtype=F32) / half)
    ang = pos.astype(F32)[:, None] * inv_freq[None, :]
    reps = A_WIDTH // half
    return jnp.tile(jnp.cos(ang), (1, reps)), jnp.tile(jnp.sin(ang), (1, reps))


def _prep_mixer_params(norm1_g, w_in, a_q_norm, a_k_norm, g_gate_w2, g_gate_b):
    glr_off = 3 * A_WIDTH + 2 * G_KEY_WIDTH + G_VAL_WIDTH
    w_pad = jnp.concatenate([
        w_in[:, :glr_off],
        jnp.pad(w_in[:, glr_off:glr_off + G_GATE_RANK], ((0, 0), (0, GLR_PAD - G_GATE_RANK))),
        w_in[:, glr_off + G_GATE_RANK:]], axis=1).astype(BF16)
    head_id = jnp.arange(A_WIDTH) // A_HEAD_DIM
    return {
        "g1": norm1_g.reshape(1, D_MODEL),
        "w_in": w_pad,
        "qn": jnp.tile(a_q_norm, A_HEADS).reshape(1, A_WIDTH),
        "kn": jnp.tile(a_k_norm, A_HEADS).reshape(1, A_WIDTH),
        "bd": (head_id[:, None] == head_id[None, :]).astype(BF16),
        "w2": jnp.pad(g_gate_w2, ((0, GLR_PAD - G_GATE_RANK), (0, 0))).astype(BF16),
        "gb": g_gate_b.reshape(1, G_KEY_WIDTH),
    }


def _topk_rows_mask(s, valid, n_sel):
    n = s.shape[0]
    sm = jnp.where(valid, s, -jnp.inf)
    ridx = lax.broadcasted_iota(jnp.int32, s.shape, 0)
    cnt = jnp.zeros(s.shape, jnp.int32)
    for r in range(n):
        row = sm[r:r + 1, :]
        beats = (row > sm) | ((row == sm) & (r < ridx))
        cnt = cnt + beats.astype(jnp.int32)
    return valid & (cnt < n_sel)


PROMPT_BLOCKS_PER_ITER = 4


def _moba_prompt_kernel(q_ref, k_ref, v_ref, km_ref, o_ref, sel_ref):
    ob = pl.program_id(2)
    blk = MOBA_BLOCK
    nb = km_ref.shape[0]
    scale = A_HEAD_DIM ** -0.5
    kpos = lax.broadcasted_iota(jnp.int32, (blk, blk), 0)
    qpos = lax.broadcasted_iota(jnp.int32, (blk, blk), 1)
    causal = kpos <= qpos
    n_heads = LANES // A_HEAD_DIM
    head_lanes = [slice(hh * A_HEAD_DIM, (hh + 1) * A_HEAD_DIM) for hh in range(n_heads)]
    nidx = lax.broadcasted_iota(jnp.int32, (nb, blk), 0)
    qs = []
    for hh, sl in enumerate(head_lanes):
        q = q_ref[:, sl]
        s_sel = lax.dot_general(km_ref[:, sl], q, (((1,), (1,)), ((), ())),
                                precision=lax.Precision.HIGHEST, preferred_element_type=F32)
        sel_ref[hh] = _topk_rows_mask(s_sel, nidx < ob, MOBA_TOPK).astype(F32)
        qs.append((q * scale).astype(BF16))

    def scores(n, hh):
        kb = k_ref[pl.ds(n * blk, blk), head_lanes[hh]].astype(BF16)
        return _nt(kb, qs[hh])

    def values_t(n, hh):
        return v_ref[pl.ds(n * blk, blk), head_lanes[hh]].T.astype(BF16)

    init = []
    for hh in range(n_heads):
        s0 = jnp.where(causal, scores(ob, hh), -jnp.inf)
        m0 = jnp.max(s0, axis=0, keepdims=True)
        p0 = jnp.exp(s0 - m0)
        init.append((m0, jnp.sum(p0, axis=0, keepdims=True), _dot(values_t(ob, hh), p0.astype(BF16))))

    def body(it, carry):
        new = []
        for hh in range(n_heads):
            m, l, acc = carry[hh]
            blocks = [jnp.minimum(it * PROMPT_BLOCKS_PER_ITER + j, nb - 1)
                      for j in range(PROMPT_BLOCKS_PER_ITER)]
            ss = [jnp.where(sel_ref[hh, pl.ds(n, 1), :] > 0.5, scores(n, hh), -jnp.inf) for n in blocks]
            m_new = m
            for s in ss:
                m_new = jnp.maximum(m_new, jnp.max(s, axis=0, keepdims=True))
            alpha = jnp.exp(m - m_new)
            l = alpha * l
            acc = alpha * acc
            for n, s in zip(blocks, ss):
                p = jnp.exp(s - m_new)
                l = l + jnp.sum(p, axis=0, keepdims=True)
                acc = acc + _dot(values_t(n, hh), p.astype(BF16))
            new.append((m_new, l, acc))
        return tuple(new)

    n_iter = (ob + PROMPT_BLOCKS_PER_ITER - 1) // PROMPT_BLOCKS_PER_ITER
    final = lax.fori_loop(0, n_iter, body, tuple(init))
    o_ref[...] = jnp.concatenate([(acc / l).T for _, l, acc in final], axis=1)


def _moba_prompt(q, k, v, kmeans, batch, seq):
    nb = seq // MOBA_BLOCK
    hp = A_WIDTH // LANES
    return pl.pallas_call(
        _moba_prompt_kernel,
        grid=(batch, hp, nb),
        in_specs=[
            pl.BlockSpec((MOBA_BLOCK, LANES), lambda b, h, i: (b * nb + i, h)),
            pl.BlockSpec((seq, LANES), lambda b, h, i: (b, h)),
            pl.BlockSpec((seq, LANES), lambda b, h, i: (b, h)),
            pl.BlockSpec((nb, LANES), lambda b, h, i: (b, h)),
        ],
        out_specs=pl.BlockSpec((MOBA_BLOCK, LANES), lambda b, h, i: (b * nb + i, h)),
        out_shape=jax.ShapeDtypeStruct((batch * seq, A_WIDTH), F32),
        scratch_shapes=[pltpu.VMEM((LANES // A_HEAD_DIM, nb, MOBA_BLOCK), F32)],
        compiler_params=pltpu.CompilerParams(
            dimension_semantics=("arbitrary", "arbitrary", "arbitrary"), vmem_limit_bytes=VMEM_LIMIT),
        name="moba_prompt",
    )(q, k, v, kmeans)


def _cumsum_rows(tri, g):
    hi, mid, lo = _split3(g)
    return _dot(tri, hi) + _dot(tri, mid) + _dot(tri, lo)


GLA_TILE = 128


def _gla_prompt_kernel(q_ref, k_ref, v_ref, g_ref, tri_ref, o_ref, s_ref, st_ref):
    t = pl.program_id(0)
    c, sub = G_CHUNK, G_SUB
    batch = q_ref.shape[0]

    @pl.when(t == 0)
    def _():
        st_ref[...] = jnp.zeros_like(st_ref)

    tri = tri_ref[...]
    key_head = lax.broadcasted_iota(jnp.int32, (1, G_KEY_WIDTH), 1) // G_DK
    blockdiag = (lax.broadcasted_iota(jnp.int32, (G_VAL_WIDTH, G_KEY_WIDTH), 0) // G_DV
                 == lax.broadcasted_iota(jnp.int32, (G_VAL_WIDTH, G_KEY_WIDTH), 1) // G_DK)
    for ci, bi in [(ci, bi) for ci in range(GLA_TILE // c) for bi in range(batch)]:
        rows = slice(ci * c, (ci + 1) * c)
        q, k, v, g = q_ref[bi, rows, :], k_ref[bi, rows, :], v_ref[bi, rows, :], g_ref[bi, rows, :]
        b = _cumsum_rows(tri, g)
        blast = b[c - 1:c, :]
        q_inter = (q * jnp.exp(b)).astype(BF16)
        k_dec = (k * jnp.exp(blast - b)).astype(BF16)
        vb = v.astype(BF16)
        st = st_ref[bi]
        o_rows = []
        for i in range(c // sub):
            bref = b[i * sub - 1:i * sub, :] if i > 0 else jnp.zeros((1, G_KEY_WIDTH), F32)
            r = slice(i * sub, (i + 1) * sub)
            ncol = (i + 1) * sub
            q_sub = q[r, :] * jnp.exp(b[r, :] - bref)
            k_sub = (k[:ncol, :] * jnp.exp(bref - b[:ncol, :])).astype(BF16)
            q_stack = jnp.concatenate(
                [jnp.where(key_head == h, q_sub, 0.0) for h in range(G_HEADS)], axis=0).astype(BF16)
            a = _nt(q_stack, k_sub)
            rr = lax.broadcasted_iota(jnp.int32, (G_HEADS * sub, ncol), 0) % sub + i * sub
            cc = lax.broadcasted_iota(jnp.int32, (G_HEADS * sub, ncol), 1)
            av = _dot(jnp.where(cc <= rr, a, 0.0).astype(BF16), vb[:ncol, :])
            o_rows.append(jnp.concatenate(
                [av[h * sub:(h + 1) * sub, h * G_DV:(h + 1) * G_DV] for h in range(G_HEADS)], axis=1))
        o_ref[bi, rows, :] = _nt(q_inter, st.astype(BF16)) + jnp.concatenate(o_rows, axis=0)
        st_ref[bi] = st * jnp.exp(blast) + jnp.where(blockdiag, _tn(vb, k_dec), 0.0)

    @pl.when(t == pl.num_programs(0) - 1)
    def _():
        for bi in range(batch):
            for h in range(G_HEADS):
                s_ref[bi, h] = st_ref[bi, h * G_DV:(h + 1) * G_DV, h * G_DK:(h + 1) * G_DK].T


def _gla_prompt(gq, gk, gv, logg, batch, seq):
    tile = lambda w: pl.BlockSpec((batch, GLA_TILE, w), lambda t: (0, t, 0))
    tri = jnp.tril(jnp.ones((G_CHUNK, G_CHUNK), F32)).astype(BF16)
    shaped = lambda a: a.reshape(batch, seq, a.shape[-1])
    state_spec = pl.BlockSpec((batch, G_HEADS, G_DK, G_DV), lambda t: (0, 0, 0, 0))
    o, st = pl.pallas_call(
        _gla_prompt_kernel,
        grid=(seq // GLA_TILE,),
        in_specs=[tile(G_KEY_WIDTH), tile(G_KEY_WIDTH), tile(G_VAL_WIDTH), tile(G_KEY_WIDTH),
                  pl.BlockSpec((G_CHUNK, G_CHUNK), lambda t: (0, 0))],
        out_specs=[tile(G_VAL_WIDTH), state_spec],
        out_shape=[
            jax.ShapeDtypeStruct((batch, seq, G_VAL_WIDTH), F32),
            jax.ShapeDtypeStruct((batch, G_HEADS, G_DK, G_DV), F32),
        ],
        scratch_shapes=[pltpu.VMEM((batch, G_VAL_WIDTH, G_KEY_WIDTH), F32)],
        compiler_params=pltpu.CompilerParams(dimension_semantics=("arbitrary",), vmem_limit_bytes=VMEM_LIMIT),
        name="gla_prompt",
    )(shaped(gq), shaped(gk), shaped(gv), shaped(logg), tri)
    return o.reshape(batch * seq, G_VAL_WIDTH), st


GLA_SAMPLE_SEQS = 8


def _gla_sample_kernel(q_ref, k_ref, v_ref, g_ref, tri_ref, s0_ref, o_ref, s_ref, *, steps):
    q, k, v, g = q_ref[...], k_ref[...], v_ref[...], g_ref[...]
    b = _cumsum_rows(tri_ref[...], g)
    eb = jnp.exp(b)
    qd = (q * eb).astype(BF16)
    kinv = (k * jnp.exp(-b)).astype(BF16)
    vb = v.astype(BF16)
    rr = lax.broadcasted_iota(jnp.int32, (steps, steps), 0)
    cc = lax.broadcasted_iota(jnp.int32, (steps, steps), 1)
    eye = lax.broadcasted_iota(jnp.int32, (G_DK, G_DK), 0) == lax.broadcasted_iota(jnp.int32, (G_DK, G_DK), 1)
    out_rows = []
    for s in range(GLA_SAMPLE_SEQS):
        rows = slice(s * steps, (s + 1) * steps)
        blast = b[(s + 1) * steps - 1:(s + 1) * steps, :]
        k_dec = (k[rows, :] * jnp.exp(blast - b[rows, :])).astype(BF16)
        outs = []
        for h in range(G_HEADS):
            hs = slice(h * G_DK, (h + 1) * G_DK)
            vs = slice(h * G_DV, (h + 1) * G_DV)
            s0 = s0_ref[s, h]
            a = jnp.where(cc <= rr, _nt(qd[rows, hs], kinv[rows, hs]), 0.0).astype(BF16)
            outs.append(_dot(qd[rows, hs], s0.astype(BF16)) + _dot(a, vb[rows, vs]))
            dcol = jnp.sum(jnp.where(eye, jnp.exp(blast[:, hs]), 0.0), axis=1, keepdims=True)
            s_ref[s, h] = dcol * s0 + _tn(k_dec[:, hs], vb[rows, vs])
        out_rows.append(jnp.concatenate(outs, axis=1))
    o_ref[...] = jnp.concatenate(out_rows, axis=0)


def _gla_sample(gq, gk, gv, logg, state, steps):
    nseq = state.shape[0]
    sb = GLA_SAMPLE_SEQS
    rows = sb * steps
    idx = jnp.arange(rows)
    tri = ((idx[:, None] >= idx[None, :]) & (idx[:, None] // steps == idx[None, :] // steps)).astype(BF16)
    row = lambda i: (i, 0)
    return pl.pallas_call(
        functools.partial(_gla_sample_kernel, steps=steps),
        grid=(nseq // sb,),
        in_specs=[
            pl.BlockSpec((rows, G_KEY_WIDTH), row),
            pl.BlockSpec((rows, G_KEY_WIDTH), row),
            pl.BlockSpec((rows, G_VAL_WIDTH), row),
            pl.BlockSpec((rows, G_KEY_WIDTH), row),
            pl.BlockSpec((rows, rows), lambda i: (0, 0)),
            pl.BlockSpec((sb, G_HEADS, G_DK, G_DV), lambda i: (i, 0, 0, 0)),
        ],
        out_specs=[
            pl.BlockSpec((rows, G_VAL_WIDTH), row),
            pl.BlockSpec((sb, G_HEADS, G_DK, G_DV), lambda i: (i, 0, 0, 0)),
        ],
        out_shape=[
            jax.ShapeDtypeStruct((nseq * steps, G_VAL_WIDTH), F32),
            jax.ShapeDtypeStruct(state.shape, F32),
        ],
        compiler_params=pltpu.CompilerParams(dimension_semantics=("arbitrary",), vmem_limit_bytes=VMEM_LIMIT),
        name="gla_sample",
    )(gq, gk, gv, logg, tri, state)


PAGES_PER_BLOCK = MOBA_BLOCK // PAGE_SIZE


SAMPLE_PAGES_PER_STEP = 16


def _moba_sample_kernel(pt_ref, q_ref, kn_ref, vn_ref, *refs, steps, n_blocks):
    del pt_ref
    pps = SAMPLE_PAGES_PER_STEP
    k_refs, v_refs = refs[:pps], refs[pps:2 * pps]
    o_ref, wq_ref, m_ref, l_ref, ss_ref, acc_ref = refs[2 * pps:]
    g = pl.program_id(1)
    nq = A_HEADS * steps
    scale = A_HEAD_DIM ** -0.5

    @pl.when(g == 0)
    def _():
        qt = jnp.concatenate([q_ref[...]] * A_HEADS, axis=0)
        rh = lax.broadcasted_iota(jnp.int32, (nq, A_WIDTH), 0) // steps
        ch = lax.broadcasted_iota(jnp.int32, (nq, A_WIDTH), 1) // A_HEAD_DIM
        wq_ref[...] = jnp.where(rh == ch, qt * scale, 0.0).astype(BF16)

    wq = wq_ref[...]
    for bi in range(pps // PAGES_PER_BLOCK):
        n = g * (pps // PAGES_PER_BLOCK) + bi
        pages = range(bi * PAGES_PER_BLOCK, (bi + 1) * PAGES_PER_BLOCK)
        st = jnp.concatenate([_dot(wq, k_refs[j][0].astype(BF16)).T for j in pages], axis=0)
        m = jnp.max(st, axis=0, keepdims=True)
        p = jnp.exp(st - m)
        m_ref[pl.ds(n, 1), :] = m
        l_ref[pl.ds(n, 1), :] = jnp.sum(p, axis=0, keepdims=True)
        ss_ref[pl.ds(n, 1), :] = jnp.sum(st, axis=0, keepdims=True)
        pb = p.astype(BF16)
        acc = None
        for jj, j in enumerate(pages):
            part = _dot(v_refs[j][0].astype(BF16), pb[jj * PAGE_SIZE:(jj + 1) * PAGE_SIZE, :])
            acc = part if acc is None else acc + part
        acc_ref[n] = acc

    @pl.when(g == pl.num_programs(1) - 1)
    def _():
        ss = ss_ref[...]
        sel = _topk_rows_mask(ss, jnp.ones(ss.shape, jnp.bool_), min(MOBA_TOPK, n_blocks))
        s_own = _nt(kn_ref[...].astype(BF16), wq)
        tk = lax.broadcasted_iota(jnp.int32, (steps, nq), 0)
        tq = lax.broadcasted_iota(jnp.int32, (steps, nq), 1) % steps
        s_own = jnp.where(tk <= tq, s_own, -jnp.inf)
        mm = jnp.where(sel, m_ref[...], -jnp.inf)
        big = jnp.maximum(jnp.max(mm, axis=0, keepdims=True), jnp.max(s_own, axis=0, keepdims=True))
        w = jnp.where(sel, jnp.exp(mm - big), 0.0)
        p_own = jnp.exp(s_own - big)
        denom = jnp.sum(w * l_ref[...], axis=0, keepdims=True) + jnp.sum(p_own, axis=0, keepdims=True)
        acc = _tn(vn_ref[...].astype(BF16), p_own.astype(BF16))
        for j in range(n_blocks):
            acc = acc + w[j:j + 1, :] * acc_ref[j]
        out = (acc / denom).T
        o_ref[...] = jnp.concatenate(
            [out[h * steps:(h + 1) * steps, h * A_HEAD_DIM:(h + 1) * A_HEAD_DIM] for h in range(A_HEADS)], axis=1)


def _moba_sample(q, k_new, v_new, cache_kt, cache_vt, page_table, steps):
    nseq, n_pages = page_table.shape
    n_blocks = n_pages // PAGES_PER_BLOCK
    pps = SAMPLE_PAGES_PER_STEP
    nq = A_HEADS * steps
    tok = pl.BlockSpec((steps, A_WIDTH), lambda s, g, pt: (s, 0))

    def page(j):
        return pl.BlockSpec((1, A_WIDTH, PAGE_SIZE), lambda s, g, pt: (pt[s * n_pages + g * pps + j], 0, 0))

    pages = [page(j) for j in range(pps)]
    grid_spec = pltpu.PrefetchScalarGridSpec(
        num_scalar_prefetch=1,
        grid=(nseq, n_pages // pps),
        in_specs=[tok, tok, tok] + pages + pages,
        out_specs=tok,
        scratch_shapes=[
            pltpu.VMEM((nq, A_WIDTH), BF16),
            pltpu.VMEM((n_blocks, nq), F32),
            pltpu.VMEM((n_blocks, nq), F32),
            pltpu.VMEM((n_blocks, nq), F32),
            pltpu.VMEM((n_blocks, A_WIDTH, nq), F32),
        ],
    )
    return pl.pallas_call(
        functools.partial(_moba_sample_kernel, steps=steps, n_blocks=n_blocks),
        grid_spec=grid_spec,
        out_shape=jax.ShapeDtypeStruct((nseq * steps, A_WIDTH), F32),
        compiler_params=pltpu.CompilerParams(
            dimension_semantics=("arbitrary", "arbitrary"), vmem_limit_bytes=VMEM_LIMIT),
        name="moba_sample",
    )(page_table.reshape(-1), q, k_new, v_new, *([cache_kt] * pps), *([cache_vt] * pps))


def _merge_kernel(x_ref, oa_ref, ob_ref, go_ref, ma_ref, mb_ref, gn_ref, wa_ref, wb_ref, wo_ref, h_ref):
    ya = _dot(oa_ref[...].astype(BF16), wa_ref[...])
    ob = ob_ref[...]
    normed = []
    for h in range(G_HEADS):
        seg = ob[:, h * G_DV:(h + 1) * G_DV]
        normed.append(seg * lax.rsqrt(jnp.mean(seg * seg, axis=-1, keepdims=True) + EPS))
    go = go_ref[...]
    ob = jnp.concatenate(normed, axis=1) * gn_ref[...] * (go * jax.nn.sigmoid(go))
    yb = _dot(ob.astype(BF16), wb_ref[...])
    merged = jax.nn.sigmoid(ma_ref[...]) * ya + jax.nn.sigmoid(mb_ref[...]) * yb
    h_ref[...] = x_ref[...] + _dot(merged.astype(BF16), wo_ref[...])


def _merge(x, o_a, o_b, gout, ma, mb, p):
    n = x.shape[0]
    tm = TOKEN_TILE
    row = lambda i: (i, 0)
    const = lambda i: (0, 0)
    return pl.pallas_call(
        _merge_kernel,
        grid=(n // tm,),
        in_specs=[
            pl.BlockSpec((tm, D_MODEL), row),
            pl.BlockSpec((tm, A_WIDTH), row),
            pl.BlockSpec((tm, G_VAL_WIDTH), row),
            pl.BlockSpec((tm, G_VAL_WIDTH), row),
            pl.BlockSpec((tm, D_MODEL), row),
            pl.BlockSpec((tm, D_MODEL), row),
            pl.BlockSpec((1, G_VAL_WIDTH), const),
            pl.BlockSpec((A_WIDTH, D_MODEL), const),
            pl.BlockSpec((G_VAL_WIDTH, D_MODEL), const),
            pl.BlockSpec((D_MODEL, D_MODEL), const),
        ],
        out_specs=pl.BlockSpec((tm, D_MODEL), row),
        out_shape=jax.ShapeDtypeStruct((n, D_MODEL), F32),
        compiler_params=pltpu.CompilerParams(dimension_semantics=("arbitrary",), vmem_limit_bytes=VMEM_LIMIT),
        name="merge",
    )(x, o_a, o_b, gout, ma, mb, p["gn"], p["wa"], p["wb"], p["wo"])


PEER_TOKEN_TILE = 512
PEER_KEYS_PER_STEP = 8
PEER_EXPERT_TILE = PEER_KEYS_PER_STEP * P_N_KEYS
PEER_GATE_GROUP = 4
PEER_GATE_KEY_SPLIT = 2
BIG = 1e30


def _top_rows(s, count):
    cur = s
    vals = []
    for _ in range(count):
        m = jnp.max(cur, axis=0, keepdims=True)
        vals.append(m)
        cur = jnp.where(cur >= m, -jnp.inf, cur)
    return vals


def _peer_kernel(h_ref, g2_ref, wq_ref, sk1_ref, sk2_ref, u_ref, vt_ref, o_ref,
                 hb_ref, th_ref, a_ref, s2_ref, c_ref, acc_ref, act_ref, p_ref, ths_ref, as_ref):
    e = pl.program_id(1)
    tm = h_ref.shape[0]

    @pl.when(e == 0)
    def _():
        hx = h_ref[...]
        hn = hx * lax.rsqrt(jnp.mean(hx * hx, axis=-1, keepdims=True) + EPS) * g2_ref[...]
        hb = hn.astype(BF16)
        hb_ref[...] = hb
        for hd in range(P_HEADS):
            qt = _nt(wq_ref[hd * 2 * P_HALF:(hd + 1) * 2 * P_HALF, :], hb)
            s1 = _dot(sk1_ref[hd], qt[:P_HALF].astype(BF16))
            s2 = _dot(sk2_ref[hd], qt[P_HALF:].astype(BF16))
            v1 = _top_rows(s1, P_TOPK)
            v2 = _top_rows(s2, P_TOPK)
            cand = [v1[a] + v2[b] for a in range(P_TOPK) for b in range(P_TOPK // (a + 1))]
            pad = -len(cand) % 8
            cand = jnp.concatenate(cand + [jnp.full((pad, tm), -jnp.inf, F32)], axis=0)
            tops = _top_rows(cand, P_TOPK + 1)
            cut = 0.5 * (tops[P_TOPK - 1] + tops[P_TOPK])
            z = jnp.sum(jnp.where(cand >= tops[P_TOPK - 1], jnp.exp(cand - tops[0]), 0.0),
                        axis=0, keepdims=True)
            th = jnp.where(s1 >= v1[P_TOPK - 1], cut - s1, BIG)
            aa = jnp.exp(s1 - v1[0]) / z
            s2m = jnp.where(s2 >= v2[P_TOPK - 1], s2, -BIG)
            cc2 = jnp.exp(s2 - v2[0])
            for cc in range(tm // LANES):
                cols = slice(cc * LANES, (cc + 1) * LANES)
                th_ref[hd, cc] = th[:, cols]
                a_ref[hd, cc] = aa[:, cols]
                s2_ref[hd, cc] = s2m[:, cols]
                c_ref[hd, cc, :P_N_KEYS, :] = cc2[:, cols]
        acc_ref[...] = jnp.zeros_like(acc_ref)

    n_cc = tm // LANES
    group = PEER_GATE_GROUP
    half = P_N_KEYS // PEER_GATE_KEY_SPLIT

    base = pl.multiple_of(e * PEER_KEYS_PER_STEP, PEER_KEYS_PER_STEP)
    ths_ref[...] = th_ref[:, :, pl.ds(base, PEER_KEYS_PER_STEP), :]
    as_ref[:, :, :PEER_KEYS_PER_STEP, :] = a_ref[:, :, pl.ds(base, PEER_KEYS_PER_STEP), :]

    def gate_unit(grp, cc, act_r, p_w):
        cols = slice(cc * LANES, (cc + 1) * LANES)
        for jh in range(PEER_GATE_KEY_SPLIT):
            jr = slice(jh * half, (jh + 1) * half)
            g = [jnp.zeros((half, LANES), F32) for _ in range(group)]
            for hd in range(P_HEADS):
                s2h = s2_ref[hd, cc, jr, :]
                ch = c_ref[hd, cc, jr, :]
                for k in range(group):
                    i = grp * group + k
                    th = ths_ref[hd, cc, i:i + 1, :]
                    aa = as_ref[hd, cc, i:i + 1, :]
                    g[k] = g[k] + jnp.where(s2h >= th, ch, 0.0) * aa
            for k in range(group):
                r0 = (grp * group + k) * P_N_KEYS + jh * half
                act = act_r[r0:r0 + half, cols]
                gl = 0.5 * act * (1.0 + lax.erf(act * (1.0 / math.sqrt(2.0))))
                p_w[r0:r0 + half, cols] = (g[k] * gl).astype(BF16)

    act_ref[...] = _nt(u_ref[...], hb_ref[...])
    for grp in range(PEER_KEYS_PER_STEP // group):
        for cc in range(n_cc):
            gate_unit(grp, cc, act_ref, p_ref)
    acc_ref[...] += _dot(vt_ref[0], p_ref[...])

    @pl.when(e == pl.num_programs(1) - 1)
    def _():
        o_ref[...] = h_ref[...] + acc_ref[...].T


def _peer(h, p):
    n = h.shape[0]
    tm = PEER_TOKEN_TILE
    te = PEER_EXPERT_TILE
    n_et = p["u"].shape[0] // te
    row = lambda t, e: (t, 0)
    const2 = lambda t, e: (0, 0)
    const3 = lambda t, e: (0, 0, 0)
    sel_scratch = pltpu.VMEM((P_HEADS, tm // LANES, P_N_KEYS, LANES), F32)
    return pl.pallas_call(
        _peer_kernel,
        grid=(n // tm, n_et),
        in_specs=[
            pl.BlockSpec((tm, D_MODEL), row),
            pl.BlockSpec((1, D_MODEL), const2),
            pl.BlockSpec((P_HEADS * 2 * P_HALF, D_MODEL), const2),
            pl.BlockSpec((P_HEADS, P_N_KEYS, P_HALF), const3),
            pl.BlockSpec((P_HEADS, P_N_KEYS, P_HALF), const3),
            pl.BlockSpec((te, D_MODEL), lambda t, e: (e, 0)),
            pl.BlockSpec((1, D_MODEL, te), lambda t, e: (e, 0, 0)),
        ],
        out_specs=pl.BlockSpec((tm, D_MODEL), row),
        out_shape=jax.ShapeDtypeStruct((n, D_MODEL), F32),
        scratch_shapes=[
            pltpu.VMEM((tm, D_MODEL), BF16),
            sel_scratch, sel_scratch, sel_scratch,
            pltpu.VMEM((P_HEADS, tm // LANES, P_N_KEYS + 8, LANES), F32),
            pltpu.VMEM((D_MODEL, tm), F32),
            pltpu.VMEM((te, tm), F32),
            pltpu.VMEM((te, tm), BF16),
            pltpu.VMEM((P_HEADS, tm // LANES, PEER_KEYS_PER_STEP, LANES), F32),
            pltpu.VMEM((P_HEADS, tm // LANES, 2 * PEER_KEYS_PER_STEP, LANES), F32),
        ],
        compiler_params=pltpu.CompilerParams(
            dimension_semantics=("arbitrary", "arbitrary"), vmem_limit_bytes=VMEM_LIMIT),
        name="peer",
    )(h, p["g2"], p["wq_t"], p["sk1"], p["sk2"], p["u"], p["v_t"])


def _prep_post_params(g_out_norm, w_branch_a, w_branch_b, w_out, norm2_g, peer_wq, sk1, sk2, peer_u, peer_v):
    return {
        "gn": jnp.tile(g_out_norm, G_HEADS).reshape(1, G_VAL_WIDTH),
        "wa": w_branch_a.astype(BF16),
        "wb": w_branch_b.astype(BF16),
        "wo": w_out.astype(BF16),
        "g2": norm2_g.reshape(1, D_MODEL),
        "wq_t": peer_wq.T.astype(BF16),
        "sk1": sk1.astype(BF16),
        "sk2": sk2.astype(BF16),
        "u": peer_u.astype(BF16),
        "v_t": jnp.transpose(peer_v.reshape(-1, PEER_EXPERT_TILE, D_MODEL), (0, 2, 1)).astype(BF16),
    }


def kernel(x_prompt, x_sample, cache_k, cache_v, state_gla, page_table, norm1_g, w_in, a_q_norm, a_k_norm,
           g_gate_w2, g_gate_b, g_out_norm, w_branch_a, w_branch_b, w_out, norm2_g, peer_wq,
           peer_subkeys1, peer_subkeys2, peer_u, peer_v):
    batch, seq, d = x_prompt.shape
    nseq, steps, _ = x_sample.shape
    depth, n_phys = cache_k.shape[:2]
    assert depth == 1 and d == D_MODEL
    assert seq % MOBA_BLOCK == 0 and MOBA_BLOCK == TOKEN_TILE
    past_len = page_table.shape[1] * PAGE_SIZE
    assert past_len % MOBA_BLOCK == 0 and TOKEN_TILE % steps == 0 and (nseq * steps) % TOKEN_TILE == 0
    assert math.gcd(steps, G_CHUNK) == steps and nseq % GLA_SAMPLE_SEQS == 0

    mp = _prep_mixer_params(norm1_g[0], w_in[0], a_q_norm[0], a_k_norm[0], g_gate_w2[0], g_gate_b[0])
    pp = _prep_post_params(g_out_norm[0], w_branch_a[0], w_branch_b[0], w_out[0], norm2_g[0], peer_wq[0],
                           peer_subkeys1[0], peer_subkeys2[0], peer_u[0], peer_v[0])

    nb = seq // TOKEN_TILE
    cos_p, sin_p = _rope_tables(jnp.arange(seq, dtype=jnp.int32))
    xp = x_prompt.reshape(batch * seq, d)
    q, k, v, gq, gk, gv, lg, go, ma, mb, km, k_t, v_t = _inproj(
        xp, lambda i: i % nb, cos_p, sin_p, mp, seq_tiles=nb)
    o_a = _moba_prompt(q, k, v, km.reshape(batch * nb, A_WIDTH), batch, seq)
    o_b, st_p = _gla_prompt(gq, gk, gv, lg, batch, seq)
    y_p = _peer(_merge(xp, o_a, o_b, go, ma, mb, pp), pp)

    pos_s = past_len + jnp.arange(TOKEN_TILE, dtype=jnp.int32) % steps
    cos_s, sin_s = _rope_tables(pos_s)
    xs = x_sample.reshape(nseq * steps, d)
    qs, ks, vs, gqs, gks, gvs, lgs, gos, mas, mbs, _ = _inproj(xs, lambda i: 0, cos_s, sin_s, mp)
    def pages_t(c):
        c = c.reshape(n_phys, PAGE_SIZE, A_HEADS, A_HEAD_DIM)
        return jnp.transpose(c, (0, 2, 3, 1)).reshape(n_phys, A_WIDTH, PAGE_SIZE)

    o_as = _moba_sample(qs, ks, vs, pages_t(cache_k), pages_t(cache_v), page_table, steps)
    o_bs, st_s = _gla_sample(gqs, gks, gvs, lgs, state_gla.reshape(nseq, G_HEADS, G_DK, G_DV), steps)
    y_s = _peer(_merge(xs, o_as, o_bs, gos, mas, mbs, pp), pp)

    def cache_view(a_t):
        return jnp.transpose(a_t.reshape(1, batch, A_HEADS, A_HEAD_DIM, seq), (0, 1, 4, 2, 3))

    kv_s = (1, nseq, steps, A_HEADS, A_HEAD_DIM)
    return (y_p.reshape(batch, seq, d), y_s.reshape(nseq, steps, d),
            cache_view(k_t), cache_view(v_t), st_p[None],
            ks.reshape(kv_s), vs.reshape(kv_s), st_s[None])
```
